```python
import math
import jax, jax.numpy as jnp
from jax import lax
import numpy as np

D_MODEL = 2048
BATCH = 32
SEQ = 256
DEPTH = 2
DEC_BATCH = 4
DEC_SEQ = 1024
PAST_LEN = 256

GRID_W = 64
N_MIXERS = 2
NA_HEADS = 16
NA_HEAD_DIM = D_MODEL // NA_HEADS
WIN_ROWS_MAX = 8
WIN_COLS = 16
Q_COL_BLOCK = 16
K_COL_BAND = Q_COL_BLOCK + WIN_COLS
CTX_Q_BLOCK = 128
SSM_D_INNER = 2 * D_MODEL
SSM_HEAD_DIM = 64
SSM_HEADS = SSM_D_INNER // SSM_HEAD_DIM
SSM_GROUPS = 8
SSM_STATE = 128
SSM_CONV = 3
SSM_CHUNK = 128
SSM_CONV_DIM = SSM_D_INNER + 2 * SSM_GROUPS * SSM_STATE
SSM_IN_DIM = SSM_D_INNER + SSM_CONV_DIM + 2 * SSM_HEADS
D_FF = 4 * D_MODEL
NORM_EPS = 1e-6
NEG_INF = -1e30

kernel_name = "hybrid_na_ssd_prefix_dit_step"


def rms_norm(x, g):
    xf = x.astype(jnp.float32)
    y = xf * lax.rsqrt(jnp.mean(xf * xf, axis=-1, keepdims=True) + NORM_EPS)
    return (y * g.astype(jnp.float32)).astype(x.dtype)


def ada_mod(cond, w, b):
    m = jax.nn.silu(cond) @ w + b
    return jnp.split(m[:, None, :], 6, axis=-1)


def modulate(h, shift, scale):
    return h * (1 + scale) + shift


def qkv_heads(h, w_qkv):
    b, L, _ = h.shape
    qkv = (h @ w_qkv).reshape(b, L, 3, NA_HEADS, NA_HEAD_DIM)
    return qkv[:, :, 0], qkv[:, :, 1], qkv[:, :, 2]


def ctx_self_attention(q, k, v):
    b, L, h, d = q.shape
    nb = L // CTX_Q_BLOCK
    qb = jnp.moveaxis((q * d ** -0.5).reshape(b, nb, CTX_Q_BLOCK, h, d), 1, 0)

    def block(qi):
        s = jnp.einsum('bqhd,bkhd->bhqk', qi, k).astype(jnp.float32)
        p = jax.nn.softmax(s, axis=-1).astype(v.dtype)
        return jnp.einsum('bhqk,bkhd->bqhd', p, v)

    o = lax.map(block, qb)
    return jnp.moveaxis(o, 0, 1).reshape(b, L, h, d)


def neighborhood_attention(q, k, v, ck, cv, rpb):
    b, S, h, d = q.shape
    rows = S // GRID_W
    kr = min(WIN_ROWS_MAX, rows)
    ncb = GRID_W // Q_COL_BLOCK
    r_idx = jnp.arange(rows)
    row_start = jnp.clip(r_idx - kr // 2, 0, rows - kr)
    band_start = jnp.clip(jnp.arange(ncb) * Q_COL_BLOCK - WIN_COLS // 2, 0, GRID_W - K_COL_BAND)
    band_cols = band_start[:, None] + jnp.arange(K_COL_BAND)
    q_cols = jnp.arange(ncb)[:, None] * Q_COL_BLOCK + jnp.arange(Q_COL_BLOCK)
    win_start = jnp.clip(q_cols - WIN_COLS // 2, 0, GRID_W - WIN_COLS)
    kc = band_cols[:, None, :]
    col_ok = (kc >= win_start[..., None]) & (kc < win_start[..., None] + WIN_COLS)
    dcol_idx = jnp.clip(kc - q_cols[..., None], -(WIN_COLS - 1), WIN_COLS - 1) + WIN_COLS - 1
    col_bias = rpb[:, :, dcol_idx]

    kg = k.reshape(b, rows, GRID_W, h, d)
    vg = v.reshape(b, rows, GRID_W, h, d)
    qg = jnp.moveaxis((q * d ** -0.5).reshape(b, rows, ncb, Q_COL_BLOCK, h, d), 1, 0)

    def one_row(args):
        q_r, r, rs = args
        k_blk = jnp.take(lax.dynamic_slice_in_dim(kg, rs, kr, axis=1), band_cols, axis=2)
        v_blk = jnp.take(lax.dynamic_slice_in_dim(vg, rs, kr, axis=1), band_cols, axis=2)
        s_loc = jnp.einsum('bnqhd,bjnkhd->bnqhjk', q_r, k_blk).astype(jnp.float32)
        drow_idx = rs + jnp.arange(kr) - r + WIN_ROWS_MAX - 1
        bias = jnp.take(col_bias, drow_idx, axis=1).transpose(2, 3, 0, 1, 4)
        s_loc = jnp.where(col_ok[:, :, None, None, :], s_loc + bias.astype(jnp.float32), NEG_INF)
        s_ctx = jnp.einsum('bnqhd,bchd->bnqhc', q_r, ck).astype(jnp.float32)
        nloc = kr * K_COL_BAND
        s = jnp.concatenate([s_loc.reshape(b, ncb, Q_COL_BLOCK, h, nloc), s_ctx], axis=-1)
        p = jax.nn.softmax(s, axis=-1).astype(v.dtype)
        p_loc = p[..., :nloc].reshape(b, ncb, Q_COL_BLOCK, h, kr, K_COL_BAND)
        return (jnp.einsum('bnqhjk,bjnkhd->bnqhd', p_loc, v_blk)
                + jnp.einsum('bnqhc,bchd->bnqhd', p[..., nloc:], cv))

    o = lax.map(one_row, (qg, r_idx, row_start))
    return jnp.moveaxis(o, 0, 1).reshape(b, S, h, d)


def centred_dwconv(u, w, bias):
    K = w.shape[0]
    pad = K // 2
    y = lax.conv_general_dilated(u, w[:, None, :], window_strides=(1,), padding=[(pad, K - 1 - pad)],
                                 dimension_numbers=('NWC', 'WIO', 'NWC'), feature_group_count=u.shape[-1])
    return y + bias


def ssd_scan(x, dt, a, bm, cm, d_skip, h0):
    b, L, H, P = x.shape
    G, N = bm.shape[2], bm.shape[3]
    E = H // G
    Q = SSM_CHUNK
    nc = L // Q
    f32 = jnp.float32
    xc = x.reshape(b, nc, Q, G, E, P).astype(f32)
    dtc = dt.reshape(b, nc, Q, G, E).astype(f32)
    bc = bm.reshape(b, nc, Q, G, N).astype(f32)
    cc = cm.reshape(b, nc, Q, G, N).astype(f32)
    a_cum = jnp.cumsum(dtc * a.astype(f32).reshape(G, E), axis=2)
    seg = a_cum[:, :, :, None] - a_cum[:, :, None, :]
    lower = jnp.tril(jnp.ones((Q, Q), dtype=bool))[None, None, :, :, None, None]
    decay = jnp.exp(jnp.where(lower, seg, NEG_INF))
    cb = jnp.einsum('bclgn,bcsgn->bclsg', cc, bc)
    y_diag = jnp.einsum('bclsge,bcsgep->bclgep', cb[..., None] * decay * dtc[:, :, None], xc)
    decay_end = jnp.exp(a_cum[:, :, -1:] - a_cum)
    chunk_states = jnp.einsum('bcsgn,bcsge,bcsgep->bcgepn', bc, decay_end * dtc, xc)
    chunk_decay = jnp.exp(a_cum[:, :, -1])

    def step(hs, inp):
        s_c, d_c = inp
        return hs * d_c[..., None, None] + s_c, hs

    h_fin, h_prev = lax.scan(step, h0.reshape(b, G, E, P, N).astype(f32),
                             (jnp.moveaxis(chunk_states, 1, 0), jnp.moveaxis(chunk_decay, 1, 0)))
    h_prev = jnp.moveaxis(h_prev, 0, 1)
    y_off = jnp.einsum('bclgn,bcgepn,bclge->bclgep', cc, h_prev, jnp.exp(a_cum))
    y = y_diag + y_off + xc * d_skip.astype(f32).reshape(G, E, 1)
    return y.reshape(b, L, H, P).astype(x.dtype), h_fin.reshape(b, H, P, N).astype(x.dtype)


def ssd_mixer(h, w_in, conv_w, conv_b, dt_bias, a_log, d_skip, norm_g, w_out, h0):
    b, L, _ = h.shape
    proj = h @ w_in
    z = proj[..., :SSM_D_INNER]
    xbc = jax.nn.silu(centred_dwconv(proj[..., SSM_D_INNER:SSM_D_INNER + SSM_CONV_DIM], conv_w, conv_b))
    dt_raw = proj[..., SSM_D_INNER + SSM_CONV_DIM:].reshape(b, L, 2, SSM_HEADS)
    gn = SSM_GROUPS * SSM_STATE
    xs = xbc[..., :SSM_D_INNER].reshape(b, L, SSM_HEADS, SSM_HEAD_DIM)
    bm = xbc[..., SSM_D_INNER:SSM_D_INNER + gn].reshape(b, L, SSM_GROUPS, SSM_STATE)
    cm = xbc[..., SSM_D_INNER + gn:].reshape(b, L, SSM_GROUPS, SSM_STATE)
    dt = jax.nn.softplus(dt_raw.astype(jnp.float32) + dt_bias.astype(jnp.float32))
    a = -jnp.exp(a_log.astype(jnp.float32))
    y_f, h_f = ssd_scan(xs, dt[:, :, 0], a[0], bm, cm, d_skip[0], h0[:, 0])
    fl = lambda t: jnp.flip(t, axis=1)
    y_b, h_b = ssd_scan(fl(xs), fl(dt[:, :, 1]), a[1], fl(bm), fl(cm), d_skip[1], h0[:, 1])
    y = (y_f + fl(y_b)).reshape(b, L, SSM_D_INNER)
    y = rms_norm(y * jax.nn.silu(z), norm_g)
    return y @ w_out, jnp.stack([h_f, h_b], axis=1)


def setup_inputs(seed: int = 0) -> dict:
    key = jax.random.key(seed)
    ks = iter(jax.random.split(key, 40))
    f32 = jnp.float32
    n_attn = (DEPTH + 1) // 2
    n_ssm = DEPTH // 2
    nrm = lambda shape, s: jax.random.normal(next(ks), shape, f32) * s
    gain = lambda shape: 1.0 + nrm(shape, 0.02)
    dt0 = jnp.exp(jax.random.uniform(next(ks), (n_ssm, 2, SSM_HEADS), f32, math.log(1e-3), math.log(1e-1)))
    return {
        "x_prompt": nrm((BATCH, SEQ, D_MODEL), 1.0),
        "x_sample": nrm((DEC_BATCH, DEC_SEQ, D_MODEL), 1.0),
        "cache_attn_k": nrm((DEC_BATCH, n_attn, PAST_LEN, NA_HEADS, NA_HEAD_DIM), 1.0),
        "cache_attn_v": nrm((DEC_BATCH, n_attn, PAST_LEN, NA_HEADS, NA_HEAD_DIM), 1.0),
        "state_ssm": nrm((DEC_BATCH, n_ssm, 2, SSM_HEADS, SSM_HEAD_DIM, SSM_STATE), 0.3),
        "c": nrm((DEC_BATCH, D_MODEL), 1.0),
        "c_ctx": nrm((D_MODEL,), 1.0),
        "mod_w": nrm((DEPTH, D_MODEL, 6 * D_MODEL), 0.5 * D_MODEL ** -0.5),
        "mod_b": nrm((DEPTH, 6 * D_MODEL), 0.02),
        "norm_mix": gain((DEPTH, D_MODEL)),
        "norm_mlp": gain((DEPTH, D_MODEL)),
        "mlp_w1": nrm((DEPTH, D_MODEL, D_FF), D_MODEL ** -0.5),
        "mlp_w2": nrm((DEPTH, D_FF, D_MODEL), D_FF ** -0.5),
        "na_wqkv": nrm((n_attn, D_MODEL, 3 * D_MODEL), D_MODEL ** -0.5),
        "na_wo": nrm((n_attn, D_MODEL, D_MODEL), D_MODEL ** -0.5),
        "na_rpb": nrm((n_attn, NA_HEADS, 2 * WIN_ROWS_MAX - 1, 2 * WIN_COLS - 1), 0.1),
        "ssm_w_in": nrm((n_ssm, D_MODEL, SSM_IN_DIM), D_MODEL ** -0.5),
        "ssm_conv_w": nrm((n_ssm, SSM_CONV, SSM_CONV_DIM), SSM_CONV ** -0.5),
        "ssm_conv_b": nrm((n_ssm, SSM_CONV_DIM), 0.02),
        "ssm_dt_bias": dt0 + jnp.log(-jnp.expm1(-dt0)),
        "ssm_a_log": jnp.log(jax.random.uniform(next(ks), (n_ssm, 2, SSM_HEADS), f32, 1.0, 16.0)),
        "ssm_d": 1.0 + nrm((n_ssm, 2, SSM_HEADS), 0.1),
        "ssm_norm": gain((n_ssm, SSM_D_INNER)),
        "ssm_w_out": nrm((n_ssm, SSM_D_INNER, D_MODEL), SSM_D_INNER ** -0.5),
        "final_norm": gain((D_MODEL,)),
    }


def reference(x_prompt, x_sample, cache_attn_k, cache_attn_v, state_ssm, c, c_ctx, mod_w, mod_b,
              norm_mix, norm_mlp, mlp_w1, mlp_w2, na_wqkv, na_wo, na_rpb, ssm_w_in, ssm_conv_w,
              ssm_conv_b, ssm_dt_bias, ssm_a_log, ssm_d, ssm_norm, ssm_w_out, final_norm):
    xp, xs = x_prompt, x_sample
    bp, lp, _ = xp.shape
    bs, ls, _ = xs.shape
    new_k, new_v, new_ssm = [], [], []
    for i in range(DEPTH):
        j = i // N_MIXERS
        mp = ada_mod(c_ctx[None, :], mod_w[i], mod_b[i])
        ms = ada_mod(c, mod_w[i], mod_b[i])
        hp = modulate(rms_norm(xp, norm_mix[i]), mp[0], mp[1])
        hs = modulate(rms_norm(xs, norm_mix[i]), ms[0], ms[1])
        if i % N_MIXERS == 0:
            qp, kp, vp = qkv_heads(hp, na_wqkv[j])
            op = ctx_self_attention(qp, kp, vp).reshape(bp, lp, D_MODEL) @ na_wo[j]
            qs, ks_, vs = qkv_heads(hs, na_wqkv[j])
            os_ = neighborhood_attention(qs, ks_, vs, cache_attn_k[:, j], cache_attn_v[:, j],
                                         na_rpb[j]).reshape(bs, ls, D_MODEL) @ na_wo[j]
            new_k.append(kp)
            new_v.append(vp)
        else:
            h0 = jnp.zeros((bp, 2, SSM_HEADS, SSM_HEAD_DIM, SSM_STATE), xp.dtype)
            op, st = ssd_mixer(hp, ssm_w_in[j], ssm_conv_w[j], ssm_conv_b[j], ssm_dt_bias[j], ssm_a_log[j],
                               ssm_d[j], ssm_norm[j], ssm_w_out[j], h0)
            os_, _ = ssd_mixer(hs, ssm_w_in[j], ssm_conv_w[j], ssm_conv_b[j], ssm_dt_bias[j], ssm_a_log[j],
                               ssm_d[j], ssm_norm[j], ssm_w_out[j], state_ssm[:, j])
            new_ssm.append(st)
        xp = xp + mp[2] * op
        xs = xs + ms[2] * os_
        hp = modulate(rms_norm(xp, norm_mlp[i]), mp[3], mp[4])
        hs = modulate(rms_norm(xs, norm_mlp[i]), ms[3], ms[4])
        xp = xp + mp[5] * (jnp.square(jax.nn.relu(hp @ mlp_w1[i])) @ mlp_w2[i])
        xs = xs + ms[5] * (jnp.square(jax.nn.relu(hs @ mlp_w1[i])) @ mlp_w2[i])
    y_prompt = rms_norm(xp, final_norm)
    y_sample = rms_norm(xs, final_norm)
    new_attn_k = jnp.stack(new_k, axis=1)
    new_attn_v = jnp.stack(new_v, axis=1)
    new_ssm_state = jnp.stack(new_ssm, axis=1)
    return (y_prompt, y_sample, new_attn_k, new_attn_v, new_ssm_state)
```

```python
import functools

import jax
import jax.numpy as jnp
from jax import lax
from jax.experimental import pallas as pl
from jax.experimental.pallas import tpu as pltpu

F32 = jnp.float32
BF16 = jnp.bfloat16

D_MODEL = 2048
BATCH = 32
SEQ = 256
DEPTH = 2
DEC_BATCH = 4
DEC_SEQ = 1024
PAST_LEN = 256
GRID_W = 64
NA_HEADS = 16
NA_HEAD_DIM = 128
WIN_ROWS = 8
WIN_COLS = 16
SSM_D_INNER = 4096
SSM_HEAD_DIM = 64
SSM_HEADS = 64
SSM_GROUPS = 8
SSM_STATE = 128
SSM_CHUNK = 128
SSM_CONV_DIM = SSM_D_INNER + 2 * SSM_GROUPS * SSM_STATE
D_FF = 4 * D_MODEL
NORM_EPS = 1e-6
NEG_INF = -1e30

N_CTX = BATCH * SEQ
N_LAT = DEC_BATCH * DEC_SEQ
N_TOK = N_CTX + N_LAT
COND_ROWS = 8
HEADS_PER_GROUP = SSM_HEADS // SSM_GROUPS
GROUP_WIDTH = HEADS_PER_GROUP * SSM_HEAD_DIM

VMEM_LIMIT_BYTES = 56 * 1024 * 1024


def _params(n_axes):
    return pltpu.CompilerParams(dimension_semantics=("arbitrary",) * n_axes,
                                vmem_limit_bytes=VMEM_LIMIT_BYTES)


def _sigmoid(x):
    return 1.0 / (1.0 + jnp.exp(-x))


def _softplus(x):
    return jnp.maximum(x, 0.0) + jnp.log(1.0 + jnp.exp(-jnp.abs(x)))


def _cond_row(row_start):
    return jnp.where(row_start < N_CTX, 0, 1 + (row_start - N_CTX) // DEC_SEQ)


def _mod_kernel(cond_ref, w_ref, b_ref, o_ref):
    c = cond_ref[...]
    s = (c * _sigmoid(c)).astype(BF16)
    o_ref[...] = jnp.dot(s, w_ref[...].astype(BF16), preferred_element_type=F32) + b_ref[...]


def mod_table(cond, mod_w, mod_b):
    tn = 1024
    n6 = 6 * D_MODEL
    out = pl.pallas_call(
        _mod_kernel,
        grid=(DEPTH, n6 // tn),
        in_specs=[pl.BlockSpec((COND_ROWS, D_MODEL), lambda l, n: (0, 0)),
                  pl.BlockSpec((None, D_MODEL, tn), lambda l, n: (l, 0, n)),
                  pl.BlockSpec((None, 1, tn), lambda l, n: (l, 0, n))],
        out_specs=pl.BlockSpec((None, COND_ROWS, tn), lambda l, n: (l, 0, n)),
        out_shape=jax.ShapeDtypeStruct((DEPTH, COND_ROWS, n6), F32),
        compiler_params=_params(2),
        name="mod_table",
    )(cond, mod_w, mod_b.reshape(DEPTH, 1, n6))
    return out.reshape(DEPTH, COND_ROWS, 6, 1, D_MODEL)


def _norm_mod_kernel(x_ref, g_ref, shift_ref, scale_ref, o_ref):
    x = x_ref[...]
    ms = jnp.mean(x * x, axis=-1, keepdims=True)
    y = x * lax.rsqrt(ms + NORM_EPS) * g_ref[...]
    o_ref[...] = (y * (1.0 + scale_ref[...]) + shift_ref[...]).astype(o_ref.dtype)


def norm_mod(x, gain, modt, layer, which_shift, which_scale, tm=512):
    n = x.shape[0]
    mod_spec = lambda which: pl.BlockSpec(
        (None, None, None, 1, D_MODEL), lambda m: (layer, _cond_row(m * tm), which, 0, 0))
    return pl.pallas_call(
        _norm_mod_kernel,
        grid=(n // tm,),
        in_specs=[pl.BlockSpec((tm, D_MODEL), lambda m: (m, 0)),
                  pl.BlockSpec((None, 1, D_MODEL), lambda m: (layer, 0, 0)),
                  mod_spec(which_shift), mod_spec(which_scale)],
        out_specs=pl.BlockSpec((tm, D_MODEL), lambda m: (m, 0)),
        out_shape=jax.ShapeDtypeStruct((n, D_MODEL), BF16),
        compiler_params=_params(1),
        name="norm_mod",
    )(x, gain.reshape(-1, 1, D_MODEL), modt, modt)


def _final_norm_kernel(x_ref, g_ref, o_ref):
    x = x_ref[...]
    ms = jnp.mean(x * x, axis=-1, keepdims=True)
    o_ref[...] = x * lax.rsqrt(ms + NORM_EPS) * g_ref[...]


def final_rmsnorm(x, gain, row_off, rows, tm=512):
    off = row_off // tm
    return pl.pallas_call(
        _final_norm_kernel,
        grid=(rows // tm,),
        in_specs=[pl.BlockSpec((tm, D_MODEL), lambda m: (off + m, 0)),
                  pl.BlockSpec((1, D_MODEL), lambda m: (0, 0))],
        out_specs=pl.BlockSpec((tm, D_MODEL), lambda m: (m, 0)),
        out_shape=jax.ShapeDtypeStruct((rows, D_MODEL), F32),
        compiler_params=_params(1),
        name="final_norm",
    )(x, gain.reshape(1, D_MODEL))


def _mm_kernel(*refs, nk, epilogue, scale):
    if epilogue == "resid":
        a_ref, w_ref, x_ref, gate_ref, o_ref, wb_ref, acc_ref = refs
    else:
        a_ref, w_ref, o_ref, wb_ref, acc_ref = refs
    m = pl.program_id(1)
    k = pl.program_id(2)

    @pl.when(m == 0)
    def _():
        wb_ref[k] = w_ref[...].astype(BF16)

    part = jnp.dot(a_ref[...], wb_ref[k], preferred_element_type=F32)

    def finish(acc):
        if epilogue == "cast":
            if scale is not None:
                acc = acc * scale
            o_ref[...] = acc.astype(o_ref.dtype)
        elif epilogue == "relu2":
            r = jnp.maximum(acc, 0.0)
            o_ref[...] = (r * r).astype(o_ref.dtype)
        else:
            o_ref[...] = x_ref[...] + gate_ref[...] * acc

    if nk == 1:
        finish(part)
    else:
        @pl.when(k == 0)
        def _():
            acc_ref[...] = part

        @pl.when(jnp.logical_and(k > 0, k < nk - 1))
        def _():
            acc_ref[...] += part

        @pl.when(k == nk - 1)
        def _():
            finish(acc_ref[...] + part)


def matmul(a, w, *, layer, col_off, cols, tn, out_dtype, tm=1024, tk=2048, row_off=0, rows=None,
           epilogue="cast", scale=None, x=None, modt=None, mod_layer=None, which=None):
    kdim = a.shape[1]
    rows = a.shape[0] - row_off if rows is None else rows
    nk, nn, nm = kdim // tk, cols // tn, rows // tm
    m_off, n_off = row_off // tm, col_off // tn
    in_specs = [
        pl.BlockSpec((tm, tk), lambda n, m, k: (m_off + m, k)),
        pl.BlockSpec((None, tk, tn), lambda n, m, k: (layer, jnp.where(m == 0, k, nk - 1), n_off + n)),
    ]
    args = [a, w]
    if epilogue == "resid":
        in_specs += [
            pl.BlockSpec((tm, tn), lambda n, m, k: (m_off + m, n)),
            pl.BlockSpec((None, None, None, 1, tn),
                         lambda n, m, k: (mod_layer, _cond_row((m_off + m) * tm), which, 0, n)),
        ]
        args += [x, modt]
    return pl.pallas_call(
        functools.partial(_mm_kernel, nk=nk, epilogue=epilogue, scale=scale),
        grid=(nn, nm, nk),
        in_specs=in_specs,
        out_specs=pl.BlockSpec((tm, tn), lambda n, m, k: (m, n)),
        out_shape=jax.ShapeDtypeStruct((rows, cols), out_dtype),
        scratch_shapes=[pltpu.VMEM((nk, tk, tn), BF16), pltpu.VMEM((tm, tn), F32)],
        compiler_params=_params(3),
        name="matmul_" + epilogue,
    )(*args)


def _ctx_attn_kernel(q_ref, k_ref, v_ref, o_ref):
    for h in range(NA_HEADS):
        sl = slice(h * NA_HEAD_DIM, (h + 1) * NA_HEAD_DIM)
        q = q_ref[:, sl]
        k = k_ref[:, sl].astype(BF16)
        v = v_ref[:, sl].astype(BF16)
        s = lax.dot_general(q, k, (((1,), (1,)), ((), ())), preferred_element_type=F32)
        p = jnp.exp(s - jnp.max(s, axis=-1, keepdims=True))
        l = jnp.sum(p, axis=-1, keepdims=True)
        o = jnp.dot(p.astype(BF16), v, preferred_element_type=F32) / l
        o_ref[:, sl] = o.astype(o_ref.dtype)


def ctx_attention(q, k, v):
    spec = pl.BlockSpec((SEQ, D_MODEL), lambda b: (b, 0))
    return pl.pallas_call(
        _ctx_attn_kernel,
        grid=(BATCH,),
        in_specs=[spec, spec, spec],
        out_specs=spec,
        out_shape=jax.ShapeDtypeStruct((N_CTX, D_MODEL), BF16),
        compiler_params=_params(1),
        name="ctx_attention",
    )(q, k, v)


GRID_ROWS = DEC_SEQ // GRID_W
ROW_KEYS = WIN_ROWS * GRID_W


def _na_bias_table(rpb):
    qc = jnp.arange(GRID_W)[:, None]
    kc = jnp.arange(GRID_W)[None, :]
    ws = jnp.clip(qc - WIN_COLS // 2, 0, GRID_W - WIN_COLS)
    ok = (kc >= ws) & (kc < ws + WIN_COLS)
    dcol = jnp.clip(kc - qc, -(WIN_COLS - 1), WIN_COLS - 1) + WIN_COLS - 1
    t = jnp.where(ok, rpb[:, :, dcol], NEG_INF)
    return jnp.concatenate([t[:, :-1], t[:, 1:]], axis=-1)


def _na_kernel(q_ref, k_ref, v_ref, ck_ref, cv_ref, t_ref, o_ref):
    ck = ck_ref[...].astype(BF16)
    cv = cv_ref[...].astype(BF16)
    nt = (((1,), (1,)), ((), ()))
    for r in range(GRID_ROWS):
        rs = min(max(r - WIN_ROWS // 2, 0), GRID_ROWS - WIN_ROWS)
        q = q_ref[r * GRID_W:(r + 1) * GRID_W, :]
        ks = k_ref[rs * GRID_W:rs * GRID_W + ROW_KEYS, :]
        vs = v_ref[rs * GRID_W:rs * GRID_W + ROW_KEYS, :]
        i0 = rs - r + WIN_ROWS - 1
        bias = jnp.concatenate([t_ref[i0 + 2 * j] for j in range(WIN_ROWS // 2)], axis=1)
        s_loc = lax.dot_general(q, ks, nt, preferred_element_type=F32) + bias
        s_ctx = lax.dot_general(q, ck, nt, preferred_element_type=F32)
        mx = jnp.maximum(jnp.max(s_loc, axis=-1, keepdims=True), jnp.max(s_ctx, axis=-1, keepdims=True))
        p_loc = jnp.exp(s_loc - mx)
        p_ctx = jnp.exp(s_ctx - mx)
        l = jnp.sum(p_loc, axis=-1, keepdims=True) + jnp.sum(p_ctx, axis=-1, keepdims=True)
        o = (jnp.dot(p_loc.astype(BF16), vs, preferred_element_type=F32)
             + jnp.dot(p_ctx.astype(BF16), cv, preferred_element_type=F32)) / l
        o_ref[r * GRID_W:(r + 1) * GRID_W, :] = o.astype(o_ref.dtype)


def neighborhood_attention(q, k, v, cache_k, cache_v, bias_table):
    q_off = N_CTX // DEC_SEQ
    tok = lambda off: pl.BlockSpec((DEC_SEQ, NA_HEAD_DIM), lambda h, b: (off + b, h))
    cache = pl.BlockSpec((None, PAST_LEN, NA_HEAD_DIM), lambda h, b: (b, 0, h))
    return pl.pallas_call(
        _na_kernel,
        grid=(NA_HEADS, DEC_BATCH),
        in_specs=[tok(q_off), tok(0), tok(0), cache, cache,
                  pl.BlockSpec((None, 2 * WIN_ROWS - 2, GRID_W, 2 * GRID_W), lambda h, b: (h, 0, 0, 0))],
        out_specs=tok(0),
        out_shape=jax.ShapeDtypeStruct((N_LAT, D_MODEL), BF16),
        compiler_params=_params(2),
        name="neighborhood_attention",
    )(q, k, v, cache_k, cache_v, bias_table)


def _ssd_kernel(*refs, seq_len, has_h0, emit_state):
    (x_ref, b_ref, c_ref, cwx_ref, cwb_ref, cwc_ref, cbx_ref, cbb_ref, cbc_ref,
     dtm_ref, dhm_ref, btm_ref, bhm_ref, atm_ref, ahm_ref, dsk_ref) = refs[:16]
    rest = list(refs[16:])
    h0_ref = rest.pop(0) if has_h0 else None
    y_ref = rest.pop(0)
    st_ref = rest.pop(0) if emit_state else None
    xs_ref, bs_ref, cs_ref, actm_ref, achm_ref, dttm_ref, dthm_ref, wx_ref, ht_ref = rest

    q = SSM_CHUNK
    nc = seq_len // q
    hp = lax.Precision.HIGHEST
    e_n = HEADS_PER_GROUP
    p_n = SSM_HEAD_DIM

    def conv_silu(u_ref, w_ref, bias_ref):
        u = u_ref[...].astype(F32)
        row = lax.broadcasted_iota(jnp.int32, u.shape, 0)
        prev = jnp.where(row == 0, 0.0, pltpu.roll(u, 1, axis=0))
        nxt = jnp.where(row == seq_len - 1, 0.0, pltpu.roll(u, seq_len - 1, axis=0))
        w = w_ref[...]
        y = prev * w[0:1] + u * w[1:2] + nxt * w[2:3] + bias_ref[...]
        return y * _sigmoid(y)

    xs_ref[...] = conv_silu(x_ref, cwx_ref, cbx_ref)
    bs_ref[...] = conv_silu(b_ref, cwb_ref, cbb_ref).astype(BF16)
    cs_ref[...] = conv_silu(c_ref, cwc_ref, cbc_ref).astype(BF16)

    a_tm = -jnp.exp(atm_ref[...])
    a_hm = -jnp.exp(ahm_ref[...])
    li = lax.broadcasted_iota(jnp.int32, (q, q), 0)
    si = lax.broadcasted_iota(jnp.int32, (q, q), 1)
    tril = (si <= li).astype(F32)
    triu = (si >= li).astype(F32)
    fwd_lane = lax.broadcasted_iota(jnp.int32, (q, 2 * e_n), 1) < e_n
    fwd_row = lax.broadcasted_iota(jnp.int32, (2 * e_n, q), 0) < e_n
    for c in range(nc):
        rows = slice(c * q, (c + 1) * q)
        dt_t = _softplus(dtm_ref[rows, :] + btm_ref[...])
        dttm_ref[rows, :] = dt_t
        da_t = dt_t * a_tm
        actm_ref[rows, :] = jnp.where(
            fwd_lane,
            jnp.dot(tril, da_t, precision=hp, preferred_element_type=F32),
            jnp.dot(triu, da_t, precision=hp, preferred_element_type=F32))
        dt_h = _softplus(dhm_ref[c] + bhm_ref[...])
        dthm_ref[c] = dt_h
        da_h = dt_h * a_hm
        achm_ref[c] = jnp.where(
            fwd_row,
            jnp.dot(da_h, triu, precision=hp, preferred_element_type=F32),
            jnp.dot(da_h, tril, precision=hp, preferred_element_type=F32))

    if has_h0:
        ht_ref[0] = h0_ref[0].T
        ht_ref[1] = h0_ref[1].T
    else:
        ht_ref[...] = jnp.zeros(ht_ref.shape, F32)

    lower = si <= li
    strict_lower = si < li
    diag = si == li
    dsum = dsk_ref[0:1, :] + dsk_ref[1:2, :]
    tn_dims = (((0,), (0,)), ((), ()))
    nt_dims = (((1,), (1,)), ((), ()))

    def scan_chunk(c, direction, with_diag):
        rows = pl.ds(pl.multiple_of(c * q, q), q)
        xc = xs_ref[rows, :]
        bc = bs_ref[rows, :]
        cc = cs_ref[rows, :]
        ac_t = actm_ref[rows, :]
        dt_t = dttm_ref[rows, :]
        lane0 = direction * e_n

        if with_diag:
            ac_h = achm_ref[c]
            dt_h = dthm_ref[c]
            cb = lax.dot_general(cc, bc, nt_dims, preferred_element_type=F32)
            for e in range(e_n):
                sl = slice(e * p_n, (e + 1) * p_n)
                seg_f = ac_t[:, e:e + 1] - ac_h[e:e + 1, :]
                seg_b = ac_t[:, e_n + e:e_n + e + 1] - ac_h[e_n + e:e_n + e + 1, :]
                decay = jnp.exp(jnp.where(lower, seg_f, seg_b))
                dtf = dt_h[e:e + 1, :]
                dtb = dt_h[e_n + e:e_n + e + 1, :]
                wdt = jnp.where(strict_lower, dtf, jnp.where(diag, dtf + dtb, dtb))
                mix = (cb * decay * wdt).astype(BF16)
                ye = jnp.dot(mix, xc[:, sl].astype(BF16), preferred_element_type=F32)
                y_ref[rows, sl] = ye + xc[:, sl] * dsum[:, sl]

        end_row = ac_t[q - 1:q, :] if direction == 0 else ac_t[0:1, :]
        w_t = jnp.exp(end_row - ac_t) * dt_t
        eac = jnp.exp(ac_t)
        for e in range(e_n):
            sl = slice(e * p_n, (e + 1) * p_n)
            wx_ref[:, sl] = (xc[:, sl] * w_t[:, lane0 + e:lane0 + e + 1]).astype(BF16)
        st = lax.dot_general(bc, wx_ref[...], tn_dims, preferred_element_type=F32)
        h_prev = ht_ref[direction]
        y_off = jnp.dot(cc, h_prev.astype(BF16), preferred_element_type=F32)
        end_decay = eac[q - 1:q, :] if direction == 0 else eac[0:1, :]
        for e in range(e_n):
            sl = slice(e * p_n, (e + 1) * p_n)
            y_ref[rows, sl] += y_off[:, sl] * eac[:, lane0 + e:lane0 + e + 1]
            ht_ref[direction, :, sl] = h_prev[:, sl] * end_decay[:, lane0 + e:lane0 + e + 1] + st[:, sl]

    def fwd_body(c, carry):
        scan_chunk(c, 0, True)
        return carry

    def bwd_body(i, carry):
        scan_chunk(nc - 1 - i, 1, False)
        return carry

    lax.fori_loop(0, nc, fwd_body, 0)
    lax.fori_loop(0, nc, bwd_body, 0)

    if emit_state:
        st_ref[0] = ht_ref[0].T
        st_ref[1] = ht_ref[1].T


def ssd_scan(proj, dt_tm, dt_hm, conv_w, conv_b, bias_tm, bias_hm, alog_tm, alog_hm, d_skip_wide,
             *, seq_len, n_seq, row_off, h0=None):
    has_h0 = h0 is not None
    emit_state = not has_h0
    nc = seq_len // SSM_CHUNK
    s_off = row_off // seq_len
    c_off = row_off // SSM_CHUNK
    gw = GROUP_WIDTH
    ns = SSM_STATE
    x_blk0 = SSM_D_INNER // gw
    b_blk0 = 2 * SSM_D_INNER // ns
    c_blk0 = b_blk0 + SSM_GROUPS
    in_specs = [
        pl.BlockSpec((seq_len, gw), lambda s, g: (s_off + s, x_blk0 + g)),
        pl.BlockSpec((seq_len, ns), lambda s, g: (s_off + s, b_blk0 + g)),
        pl.BlockSpec((seq_len, ns), lambda s, g: (s_off + s, c_blk0 + g)),
        pl.BlockSpec((None, 3, gw), lambda s, g: (0, 0, g)),
        pl.BlockSpec((None, 3, ns), lambda s, g: (0, 0, SSM_D_INNER // ns + g)),
        pl.BlockSpec((None, 3, ns), lambda s, g: (0, 0, SSM_D_INNER // ns + SSM_GROUPS + g)),
        pl.BlockSpec((None, 1, gw), lambda s, g: (0, 0, g)),
        pl.BlockSpec((None, 1, ns), lambda s, g: (0, 0, SSM_D_INNER // ns + g)),
        pl.BlockSpec((None, 1, ns), lambda s, g: (0, 0, SSM_D_INNER // ns + SSM_GROUPS + g)),
        pl.BlockSpec((None, seq_len, 2 * HEADS_PER_GROUP), lambda s, g: (g, s_off + s, 0)),
        pl.BlockSpec((None, nc, 2 * HEADS_PER_GROUP, SSM_CHUNK), lambda s, g: (g, s_off + s, 0, 0)),
        pl.BlockSpec((None, 1, 2 * HEADS_PER_GROUP), lambda s, g: (g, 0, 0)),
        pl.BlockSpec((None, 2 * HEADS_PER_GROUP, 1), lambda s, g: (g, 0, 0)),
        pl.BlockSpec((None, 1, 2 * HEADS_PER_GROUP), lambda s, g: (g, 0, 0)),
        pl.BlockSpec((None, 2 * HEADS_PER_GROUP, 1), lambda s, g: (g, 0, 0)),
        pl.BlockSpec((2, gw), lambda s, g: (0, g)),
    ]
    args = [proj, proj, proj, conv_w, conv_w, conv_w, conv_b, conv_b, conv_b,
            dt_tm, dt_hm, bias_tm, bias_hm, alog_tm, alog_hm, d_skip_wide]
    state_spec = pl.BlockSpec((None, 2, gw, ns), lambda s, g: (s, 0, g, 0))
    if has_h0:
        in_specs.append(state_spec)
        args.append(h0)
    y_spec = pl.BlockSpec((seq_len, gw), lambda s, g: (s, g))
    y_shape = jax.ShapeDtypeStruct((n_seq * seq_len, SSM_D_INNER), F32)
    if emit_state:
        out_specs = [y_spec, state_spec]
        out_shape = [y_shape, jax.ShapeDtypeStruct((n_seq, 2, SSM_D_INNER, ns), F32)]
    else:
        out_specs = y_spec
        out_shape = y_shape
    return pl.pallas_call(
        functools.partial(_ssd_kernel, seq_len=seq_len, has_h0=has_h0, emit_state=emit_state),
        grid=(n_seq, SSM_GROUPS),
        in_specs=in_specs,
        out_specs=out_specs,
        out_shape=out_shape,
        scratch_shapes=[
            pltpu.VMEM((seq_len, gw), F32), pltpu.VMEM((seq_len, ns), BF16), pltpu.VMEM((seq_len, ns), BF16),
            pltpu.VMEM((seq_len, 2 * HEADS_PER_GROUP), F32), pltpu.VMEM((nc, 2 * HEADS_PER_GROUP, SSM_CHUNK), F32),
            pltpu.VMEM((seq_len, 2 * HEADS_PER_GROUP), F32), pltpu.VMEM((nc, 2 * HEADS_PER_GROUP, SSM_CHUNK), F32),
            pltpu.VMEM((SSM_CHUNK, gw), BF16), pltpu.VMEM((2, ns, gw), F32)],
        compiler_params=_params(2),
        name="ssd_scan",
    )(*args)


def _gate_norm_kernel(y_ref, z_ref, g_ref, o_ref):
    z = z_ref[...].astype(F32)
    v = y_ref[...] * (z * _sigmoid(z))
    ms = jnp.mean(v * v, axis=-1, keepdims=True)
    o_ref[...] = (v * lax.rsqrt(ms + NORM_EPS) * g_ref[...]).astype(o_ref.dtype)


def gate_norm(y, proj, gain, row_off, tm=256):
    rows = y.shape[0]
    off = row_off // tm
    return pl.pallas_call(
        _gate_norm_kernel,
        grid=(rows // tm,),
        in_specs=[pl.BlockSpec((tm, SSM_D_INNER), lambda m: (m, 0)),
                  pl.BlockSpec((tm, SSM_D_INNER), lambda m: (off + m, 0)),
                  pl.BlockSpec((1, SSM_D_INNER), lambda m: (0, 0))],
        out_specs=pl.BlockSpec((tm, SSM_D_INNER), lambda m: (m, 0)),
        out_shape=jax.ShapeDtypeStruct((rows, SSM_D_INNER), BF16),
        compiler_params=_params(1),
        name="gate_norm",
    )(y, proj, gain.reshape(1, SSM_D_INNER))


def _mlp(x, modt, layer, norm_mlp, mlp_w1, mlp_w2):
    h = norm_mod(x, norm_mlp, modt, layer, 3, 4)
    a = matmul(h, mlp_w1, layer=layer, col_off=0, cols=D_FF, tn=1024, out_dtype=BF16, epilogue="relu2")
    return matmul(a, mlp_w2, layer=layer, col_off=0, cols=D_MODEL, tn=512, out_dtype=F32,
                  epilogue="resid", x=x, modt=modt, mod_layer=layer, which=5)


def kernel(x_prompt, x_sample, cache_attn_k, cache_attn_v, state_ssm, c, c_ctx, mod_w, mod_b, norm_mix, norm_mlp, mlp_w1, mlp_w2, na_wqkv, na_wo, na_rpb, ssm_w_in, ssm_conv_w, ssm_conv_b, ssm_dt_bias, ssm_a_log, ssm_d, ssm_norm, ssm_w_out, final_norm):
    x = jnp.concatenate([x_prompt.reshape(N_CTX, D_MODEL), x_sample.reshape(N_LAT, D_MODEL)], axis=0)
    cond = jnp.concatenate([c_ctx[None, :], c, jnp.zeros((COND_ROWS - 1 - DEC_BATCH, D_MODEL), F32)], axis=0)
    modt = mod_table(cond, mod_w, mod_b)

    h = norm_mod(x, norm_mix, modt, 0, 0, 1)
    qkv = functools.partial(matmul, h, na_wqkv, layer=0, cols=D_MODEL, tn=1024)
    q = qkv(col_off=0, out_dtype=BF16, scale=NA_HEAD_DIM ** -0.5)
    k_ctx = qkv(col_off=D_MODEL, out_dtype=F32, rows=N_CTX)
    v_ctx = qkv(col_off=2 * D_MODEL, out_dtype=F32, rows=N_CTX)
    k_lat = qkv(col_off=D_MODEL, out_dtype=BF16, row_off=N_CTX)
    v_lat = qkv(col_off=2 * D_MODEL, out_dtype=BF16, row_off=N_CTX)
    o_ctx = ctx_attention(q, k_ctx, v_ctx)
    o_lat = neighborhood_attention(
        q, k_lat, v_lat,
        cache_attn_k[:, 0].reshape(DEC_BATCH, PAST_LEN, D_MODEL),
        cache_attn_v[:, 0].reshape(DEC_BATCH, PAST_LEN, D_MODEL),
        _na_bias_table(na_rpb[0]))
    o = jnp.concatenate([o_ctx, o_lat], axis=0)
    x = matmul(o, na_wo, layer=0, col_off=0, cols=D_MODEL, tn=512, out_dtype=F32,
               epilogue="resid", x=x, modt=modt, mod_layer=0, which=2)
    x = _mlp(x, modt, 0, norm_mlp, mlp_w1, mlp_w2)

    h = norm_mod(x, norm_mix, modt, 1, 0, 1)
    zxbc_cols = SSM_D_INNER + SSM_CONV_DIM
    proj = matmul(h, ssm_w_in, layer=0, col_off=0, cols=zxbc_cols, tn=512, out_dtype=BF16)
    dt_raw = matmul(h, ssm_w_in, layer=0, col_off=zxbc_cols, cols=2 * SSM_HEADS, tn=128, out_dtype=F32)
    e_n = HEADS_PER_GROUP
    dt5 = dt_raw.reshape(N_TOK // SSM_CHUNK, SSM_CHUNK, 2, SSM_GROUPS, e_n)
    dt_tm = dt5.transpose(3, 0, 1, 2, 4).reshape(SSM_GROUPS, N_TOK, 2 * e_n)
    dt_hm = dt5.transpose(3, 0, 2, 4, 1).reshape(SSM_GROUPS, N_TOK // SSM_CHUNK, 2 * e_n, SSM_CHUNK)
    by_group = lambda p: p.reshape(2, SSM_GROUPS, e_n).transpose(1, 0, 2).reshape(SSM_GROUPS, 2 * e_n)
    bias_g = by_group(ssm_dt_bias[0])
    alog_g = by_group(ssm_a_log[0])
    ssd = functools.partial(
        ssd_scan, proj, dt_tm, dt_hm, ssm_conv_w, ssm_conv_b.reshape(1, 1, SSM_CONV_DIM),
        bias_g[:, None, :], bias_g[:, :, None], alog_g[:, None, :], alog_g[:, :, None],
        jnp.repeat(ssm_d[0], SSM_HEAD_DIM, axis=1))
    y_ctx, new_state = ssd(seq_len=SEQ, n_seq=BATCH, row_off=0)
    y_lat = ssd(seq_len=DEC_SEQ, n_seq=DEC_BATCH, row_off=N_CTX,
                h0=state_ssm[:, 0].reshape(DEC_BATCH, 2, SSM_D_INNER, SSM_STATE))
    yn = jnp.concatenate([gate_norm(y_ctx, proj, ssm_norm, 0), gate_norm(y_lat, proj, ssm_norm, N_CTX)], axis=0)
    x = matmul(yn, ssm_w_out, layer=0, col_off=0, cols=D_MODEL, tn=512, out_dtype=F32,
               epilogue="resid", x=x, modt=modt, mod_layer=1, which=2)
    x = _mlp(x, modt, 1, norm_mlp, mlp_w1, mlp_w2)

    y_prompt = final_rmsnorm(x, final_norm, 0, N_CTX).reshape(BATCH, SEQ, D_MODEL)
    y_sample = final_rmsnorm(x, final_norm, N_CTX, N_LAT).reshape(DEC_BATCH, DEC_SEQ, D_MODEL)
    kv_shape = (BATCH, 1, SEQ, NA_HEADS, NA_HEAD_DIM)
    return (y_prompt, y_sample, k_ctx.reshape(kv_shape), v_ctx.reshape(kv_shape),
            new_state.reshape(BATCH, 1, 2, SSM_HEADS, SSM_HEAD_DIM, SSM_STATE))
```

```python
import functools

import jax
import jax.numpy as jnp
from jax import lax
from jax.experimental import pallas as pl
from jax.experimental.pallas import tpu as pltpu

F32 = jnp.float32
BF16 = jnp.bfloat16

D_MODEL = 2048
BATCH = 32
SEQ = 256
DEPTH = 2
DEC_BATCH = 4
DEC_SEQ = 1024
PAST_LEN = 256
GRID_W = 64
NA_HEADS = 16
NA_HEAD_DIM = 128
WIN_ROWS = 8
WIN_COLS = 16
SSM_D_INNER = 4096
SSM_HEAD_DIM = 64
SSM_HEADS = 64
SSM_GROUPS = 8
SSM_STATE = 128
SSM_CHUNK = 128
SSM_CONV_DIM = SSM_D_INNER + 2 * SSM_GROUPS * SSM_STATE
D_FF = 4 * D_MODEL
NORM_EPS = 1e-6
NEG_INF = -1e30
LOG2E = 1.4426950408889634

N_CTX = BATCH * SEQ
N_LAT = DEC_BATCH * DEC_SEQ
N_TOK = N_CTX + N_LAT
COND_ROWS = 8
HEADS_PER_GROUP = SSM_HEADS // SSM_GROUPS
GROUP_WIDTH = HEADS_PER_GROUP * SSM_HEAD_DIM

VMEM_LIMIT_BYTES = 56 * 1024 * 1024


def _params(n_axes):
    return pltpu.CompilerParams(dimension_semantics=("arbitrary",) * n_axes,
                                vmem_limit_bytes=VMEM_LIMIT_BYTES)


def _silu(v):
    h = 0.5 * v
    return h + h * jnp.tanh(h)


def _softplus(x):
    return jnp.maximum(x, 0.0) + jnp.log(1.0 + jnp.exp(-jnp.abs(x)))


def _cond_row(row_start):
    return jnp.where(row_start < N_CTX, 0, 1 + (row_start - N_CTX) // DEC_SEQ)


def _mod_kernel(cond_ref, w_ref, b_ref, o_ref):
    s = _silu(cond_ref[...]).astype(BF16)
    o_ref[...] = jnp.dot(s, w_ref[...].astype(BF16), preferred_element_type=F32) + b_ref[...]


def mod_table(cond, mod_w, mod_b):
    tn = 1024
    n6 = 6 * D_MODEL
    out = pl.pallas_call(
        _mod_kernel,
        grid=(DEPTH, n6 // tn),
        in_specs=[pl.BlockSpec((COND_ROWS, D_MODEL), lambda l, n: (0, 0)),
                  pl.BlockSpec((None, D_MODEL, tn), lambda l, n: (l, 0, n)),
                  pl.BlockSpec((None, 1, tn), lambda l, n: (l, 0, n))],
        out_specs=pl.BlockSpec((None, COND_ROWS, tn), lambda l, n: (l, 0, n)),
        out_shape=jax.ShapeDtypeStruct((DEPTH, COND_ROWS, n6), F32),
        compiler_params=_params(2),
        name="mod_table",
    )(cond, mod_w, mod_b.reshape(DEPTH, 1, n6))
    return out.reshape(DEPTH, COND_ROWS, 6, 1, D_MODEL)


def _norm_mod_kernel(x_ref, g_ref, shift_ref, scale_ref, o_ref):
    x = x_ref[...]
    ms = jnp.mean(x * x, axis=-1, keepdims=True)
    y = x * lax.rsqrt(ms + NORM_EPS) * g_ref[...]
    o_ref[...] = (y * (1.0 + scale_ref[...]) + shift_ref[...]).astype(o_ref.dtype)


def norm_mod(x, gain, modt, layer, which_shift, which_scale, tok_off=0, tm=512):
    n = x.shape[0]
    mod_spec = lambda which: pl.BlockSpec(
        (None, None, None, 1, D_MODEL), lambda m: (layer, _cond_row(tok_off + m * tm), which, 0, 0))
    return pl.pallas_call(
        _norm_mod_kernel,
        grid=(n // tm,),
        in_specs=[pl.BlockSpec((tm, D_MODEL), lambda m: (m, 0)),
                  pl.BlockSpec((None, 1, D_MODEL), lambda m: (layer, 0, 0)),
                  mod_spec(which_shift), mod_spec(which_scale)],
        out_specs=pl.BlockSpec((tm, D_MODEL), lambda m: (m, 0)),
        out_shape=jax.ShapeDtypeStruct((n, D_MODEL), BF16),
        compiler_params=_params(1),
        name="norm_mod",
    )(x, gain.reshape(-1, 1, D_MODEL), modt, modt)


def _final_norm_kernel(x_ref, g_ref, o_ref):
    x = x_ref[...]
    ms = jnp.mean(x * x, axis=-1, keepdims=True)
    o_ref[...] = x * lax.rsqrt(ms + NORM_EPS) * g_ref[...]


def final_rmsnorm(x, gain, row_off, rows, tm=512):
    off = row_off // tm
    return pl.pallas_call(
        _final_norm_kernel,
        grid=(rows // tm,),
        in_specs=[pl.BlockSpec((tm, D_MODEL), lambda m: (off + m, 0)),
                  pl.BlockSpec((1, D_MODEL), lambda m: (0, 0))],
        out_specs=pl.BlockSpec((tm, D_MODEL), lambda m: (m, 0)),
        out_shape=jax.ShapeDtypeStruct((rows, D_MODEL), F32),
        compiler_params=_params(1),
        name="final_norm",
    )(x, gain.reshape(1, D_MODEL))


def _mm_kernel(*refs, nk, epilogue, scale, n_a, n_x, split):
    refs = list(refs)
    a_refs = [refs.pop(0) for _ in range(n_a)]
    w_ref = refs.pop(0)
    x_refs = [refs.pop(0) for _ in range(n_x)]
    gate_ref = refs.pop(0) if epilogue == "resid" else None
    o_ref = refs.pop(0)
    wb_ref = refs.pop(0)
    acc_ref = refs.pop(0) if nk > 1 else None
    m = pl.program_id(1)
    k = pl.program_id(2)

    def pick(parts):
        if len(parts) == 1:
            return parts[0][...]
        return jnp.where(m < split, parts[0][...], parts[1][...])

    @pl.when(m == 0)
    def _():
        wb_ref[k] = w_ref[...].astype(BF16)

    part = jnp.dot(pick(a_refs), wb_ref[k], preferred_element_type=F32)

    def finish(acc):
        if epilogue == "cast":
            if scale is not None:
                acc = acc * scale
            o_ref[...] = acc.astype(o_ref.dtype)
        elif epilogue == "relu2":
            r = jnp.maximum(acc, 0.0)
            o_ref[...] = (r * r).astype(o_ref.dtype)
        else:
            o_ref[...] = pick(x_refs) + gate_ref[...] * acc

    if nk == 1:
        finish(part)
    else:
        @pl.when(k == 0)
        def _():
            acc_ref[...] = part

        @pl.when(jnp.logical_and(k > 0, k < nk - 1))
        def _():
            acc_ref[...] += part

        @pl.when(k == nk - 1)
        def _():
            finish(acc_ref[...] + part)


def matmul(a, w, *, layer, col_off, cols, tn, out_dtype, tm=1024, tk=2048, row_off=0, rows=None,
           epilogue="cast", scale=None, x=None, modt=None, mod_layer=None, which=None):
    a_parts = a if isinstance(a, tuple) else (a,)
    x_parts = () if x is None else (x if isinstance(x, tuple) else (x,))
    kdim = a_parts[0].shape[1]
    if rows is None:
        rows = sum(p.shape[0] for p in a_parts) - row_off
    nk, nn, nm = kdim // tk, cols // tn, rows // tm
    m_off, n_off = row_off // tm, col_off // tn
    split = N_CTX // tm

    def row_specs(parts, width, col_index):
        if len(parts) == 1:
            return [pl.BlockSpec((tm, width), lambda n, m, k: (m_off + m, col_index(n, m, k, True)))]
        return [pl.BlockSpec((tm, width), lambda n, m, k: (jnp.minimum(m, split - 1), col_index(n, m, k, m < split))),
                pl.BlockSpec((tm, width), lambda n, m, k: (jnp.maximum(m - split, 0), col_index(n, m, k, m >= split)))]

    in_specs = row_specs(a_parts, tk, lambda n, m, k, live: jnp.where(live, k, 0))
    in_specs.append(
        pl.BlockSpec((None, tk, tn), lambda n, m, k: (layer, jnp.where(m == 0, k, nk - 1), n_off + n)))
    args = list(a_parts) + [w]
    if epilogue == "resid":
        in_specs += row_specs(x_parts, tn, lambda n, m, k, live: n)
        in_specs.append(pl.BlockSpec((None, None, None, 1, tn),
                                     lambda n, m, k: (mod_layer, _cond_row((m_off + m) * tm), which, 0, n)))
        args += list(x_parts) + [modt]
    scratch = [pltpu.VMEM((nk, tk, tn), BF16)]
    if nk > 1:
        scratch.append(pltpu.VMEM((tm, tn), F32))
    return pl.pallas_call(
        functools.partial(_mm_kernel, nk=nk, epilogue=epilogue, scale=scale,
                          n_a=len(a_parts), n_x=len(x_parts), split=split),
        grid=(nn, nm, nk),
        in_specs=in_specs,
        out_specs=pl.BlockSpec((tm, tn), lambda n, m, k: (m, n)),
        out_shape=jax.ShapeDtypeStruct((rows, cols), out_dtype),
        scratch_shapes=scratch,
        compiler_params=_params(3),
        name="matmul_" + epilogue,
    )(*args)


def _ctx_attn_kernel(q_ref, k_ref, v_ref, o_ref):
    for h in range(NA_HEADS):
        sl = slice(h * NA_HEAD_DIM, (h + 1) * NA_HEAD_DIM)
        q = q_ref[:, sl]
        k = k_ref[:, sl].astype(BF16)
        v = v_ref[:, sl].astype(BF16)
        s = lax.dot_general(q, k, (((1,), (1,)), ((), ())), preferred_element_type=F32)
        p = jnp.exp(s - jnp.max(s, axis=-1, keepdims=True))
        l = jnp.sum(p, axis=-1, keepdims=True)
        o = jnp.dot(p.astype(BF16), v, preferred_element_type=F32) / l
        o_ref[:, sl] = o.astype(o_ref.dtype)


def ctx_attention(q, k, v):
    spec = pl.BlockSpec((SEQ, D_MODEL), lambda b: (b, 0))
    return pl.pallas_call(
        _ctx_attn_kernel,
        grid=(BATCH,),
        in_specs=[spec, spec, spec],
        out_specs=spec,
        out_shape=jax.ShapeDtypeStruct((N_CTX, D_MODEL), BF16),
        compiler_params=_params(1),
        name="ctx_attention",
    )(q, k, v)


GRID_ROWS = DEC_SEQ // GRID_W
Q_BLOCK_ROWS = 4
KEY_WIN_ROWS = 12
N_ROW_OFFS = 2 * WIN_ROWS - 1
TILE_BOTH, TILE_SECOND, TILE_FIRST = 0, N_ROW_OFFS - 1, 2 * N_ROW_OFFS - 1
TILE_NONE = 3 * N_ROW_OFFS - 1
N_BIAS_TILES = TILE_NONE + 1


def _row_start(r):
    return min(max(r - WIN_ROWS // 2, 0), GRID_ROWS - WIN_ROWS)


def _key_win_start(rb):
    return min(max(rb * Q_BLOCK_ROWS - WIN_ROWS // 2, 0), GRID_ROWS - KEY_WIN_ROWS)


def _na_bias_table(rpb):
    reach = WIN_COLS - 1
    by_offset = jnp.concatenate([jnp.repeat(rpb[..., :1], GRID_W - 1 - reach, axis=-1), rpb,
                                 jnp.repeat(rpb[..., -1:], GRID_W - 1 - reach, axis=-1)], axis=-1)
    t = jnp.stack([by_offset[..., GRID_W - 1 - qc:2 * GRID_W - 1 - qc] for qc in range(GRID_W)], axis=-2)
    qc = jnp.arange(GRID_W)[:, None]
    kc = jnp.arange(GRID_W)[None, :]
    ws = jnp.clip(qc - WIN_COLS // 2, 0, GRID_W - WIN_COLS)
    t = jnp.where((kc >= ws) & (kc < ws + WIN_COLS), t, NEG_INF)
    masked = jnp.full_like(t, NEG_INF)
    return jnp.concatenate([
        jnp.concatenate([t[:, :-1], t[:, 1:]], axis=-1),
        jnp.concatenate([masked, t], axis=-1),
        jnp.concatenate([t, masked], axis=-1),
        jnp.concatenate([masked[:, :1], masked[:, :1]], axis=-1)], axis=1)


def _bias_tile_index(r, key_row):
    rs = _row_start(r)
    first_ok = rs <= key_row < rs + WIN_ROWS
    second_ok = rs <= key_row + 1 < rs + WIN_ROWS
    off = key_row - r + WIN_ROWS - 1
    if first_ok and second_ok:
        return TILE_BOTH + off
    if second_ok:
        return TILE_SECOND + off + 1
    if first_ok:
        return TILE_FIRST + off
    return TILE_NONE


def _na_kernel(q_ref, k_ref, v_ref, ck_ref, cv_ref, t_ref, o_ref):
    ck = ck_ref[...].astype(BF16)
    cv = cv_ref[...].astype(BF16)
    nt = (((1,), (1,)), ((), ()))
    blk = Q_BLOCK_ROWS * GRID_W
    for rb in range(GRID_ROWS // Q_BLOCK_ROWS):
        kw0 = _key_win_start(rb)
        keys = slice(kw0 * GRID_W, (kw0 + KEY_WIN_ROWS) * GRID_W)
        q = q_ref[rb * blk:(rb + 1) * blk, :]
        bias = jnp.concatenate([
            jnp.concatenate([t_ref[_bias_tile_index(rb * Q_BLOCK_ROWS + i, kw0 + 2 * p)]
                             for p in range(KEY_WIN_ROWS // 2)], axis=1)
            for i in range(Q_BLOCK_ROWS)], axis=0)
        s_loc = lax.dot_general(q, k_ref[keys, :], nt, preferred_element_type=F32) + bias
        s_ctx = lax.dot_general(q, ck, nt, preferred_element_type=F32)
        mx = jnp.maximum(jnp.max(s_loc, axis=-1, keepdims=True), jnp.max(s_ctx, axis=-1, keepdims=True))
        p_loc = jnp.exp(s_loc - mx)
        p_ctx = jnp.exp(s_ctx - mx)
        l = jnp.sum(p_loc, axis=-1, keepdims=True) + jnp.sum(p_ctx, axis=-1, keepdims=True)
        o = (jnp.dot(p_loc.astype(BF16), v_ref[keys, :], preferred_element_type=F32)
             + jnp.dot(p_ctx.astype(BF16), cv, preferred_element_type=F32)) / l
        o_ref[rb * blk:(rb + 1) * blk, :] = o.astype(o_ref.dtype)


def neighborhood_attention(q, k, v, cache_k, cache_v, cache_layer, bias_table):
    tok = pl.BlockSpec((DEC_SEQ, NA_HEAD_DIM), lambda h, b: (b, h))
    cache = pl.BlockSpec((None, PAST_LEN, NA_HEAD_DIM), lambda h, b: (b, cache_layer, h))
    return pl.pallas_call(
        _na_kernel,
        grid=(NA_HEADS, DEC_BATCH),
        in_specs=[tok, tok, tok, cache, cache,
                  pl.BlockSpec((None, N_BIAS_TILES, GRID_W, 2 * GRID_W), lambda h, b: (h, 0, 0, 0))],
        out_specs=tok,
        out_shape=jax.ShapeDtypeStruct((N_LAT, D_MODEL), BF16),
        compiler_params=_params(2),
        name="neighborhood_attention",
    )(q, k, v, cache_k, cache_v, bias_table)


def _ssd_kernel(*refs, seq_len, has_h0, emit_state):
    (x_ref, b_ref, c_ref, cwx_ref, cwb_ref, cwc_ref, cbx_ref, cbb_ref, cbc_ref,
     dt_ref, bias_ref, alog_ref, dsk_ref) = refs[:13]
    rest = list(refs[13:])
    h0_ref = rest.pop(0) if has_h0 else None
    y_ref = rest.pop(0)
    st_ref = rest.pop(0) if emit_state else None
    (xs_ref, bt_ref, cs_ref, act3_ref, rowp_ref, rowd_ref, cdb_ref, sloc_ref, hprev_ref, hcur_ref) = rest

    q = SSM_CHUNK
    nc = seq_len // q
    hp = lax.Precision.HIGHEST
    e_n = HEADS_PER_GROUP
    half = SSM_HEAD_DIM
    n_pairs = e_n // 2

    li = lax.broadcasted_iota(jnp.int32, (q, q), 0)
    si = lax.broadcasted_iota(jnp.int32, (q, q), 1)
    tril = (si <= li).astype(F32)
    triu = (si >= li).astype(F32)
    lower = si <= li
    diag = si == li
    first_half = si < half
    fwd_row = lax.broadcasted_iota(jnp.int32, (2 * e_n, q), 0) < e_n
    a_hm = -jnp.exp(alog_ref[...]) * LOG2E

    def chunk_rows(c):
        return pl.ds(c * q, q) if isinstance(c, int) else pl.ds(pl.multiple_of(c * q, q), q)

    def block_diag(pair):
        zero = jnp.zeros_like(pair)
        return jnp.concatenate([jnp.where(first_half, pair, zero), jnp.where(first_half, zero, pair)], axis=0)

    def prep(c, carry):
        rows = chunk_rows(c)
        r0 = c * q

        def conv_silu(u_ref, w_ref, bias_r):
            u = u_ref[rows, :].astype(F32)
            width = u.shape[1]
            before = u_ref[pl.ds(pl.multiple_of(jnp.maximum(r0 - 16, 0), 16), 16), :][15:16, :].astype(F32)
            after = u_ref[pl.ds(pl.multiple_of(jnp.minimum(r0 + q, seq_len - 16), 16), 16), :][0:1, :].astype(F32)
            before = jnp.where(c > 0, before, 0.0)
            after = jnp.where(c < nc - 1, after, 0.0)
            rid = lax.broadcasted_iota(jnp.int32, (q, width), 0)
            prev = jnp.where(rid == 0, before, pltpu.roll(u, 1, axis=0))
            nxt = jnp.where(rid == q - 1, after, pltpu.roll(u, q - 1, axis=0))
            w = w_ref[...] * 0.5
            h = prev * w[0:1] + u * w[1:2] + nxt * w[2:3] + bias_r[...] * 0.5
            return h + h * jnp.tanh(h)

        xc = conv_silu(x_ref, cwx_ref, cbx_ref).astype(BF16)
        xs_ref[rows, :] = xc
        cs_ref[rows, :] = conv_silu(c_ref, cwc_ref, cbc_ref).astype(BF16)
        bt = conv_silu(b_ref, cwb_ref, cbb_ref).T
        bt_ref[c] = bt

        dt_h = _softplus(dt_ref[c] + bias_ref[...])
        da_h = dt_h * a_hm
        ac_h = jnp.where(fwd_row,
                         jnp.dot(da_h, triu, precision=hp, preferred_element_type=F32),
                         jnp.dot(da_h, tril, precision=hp, preferred_element_type=F32))
        hi = ac_h.astype(BF16).astype(F32)
        mid = (ac_h - hi).astype(BF16).astype(F32)
        lo = (ac_h - hi - mid).astype(BF16).astype(F32)
        act3_ref[rows, :] = jnp.concatenate([hi, mid, lo], axis=0).T.astype(BF16)
        rowp_ref[c] = ac_h - jnp.log2(dt_h)
        rowd_ref[c] = jnp.log2(dt_h[:e_n] + dt_h[e_n:])
        ends = jnp.broadcast_to(jnp.where(fwd_row[:, 0:1], ac_h[:, q - 1:q], ac_h[:, 0:1]), (2 * e_n, q))
        wr = jnp.exp2(ends - ac_h) * dt_h
        cdb_ref[c] = jnp.exp2(ends)

        for j in range(n_pairs):
            lanes = slice(j * 2 * half, (j + 1) * 2 * half)
            xblk = block_diag(xc[:, lanes])
            e1, e2 = 2 * j, 2 * j + 1
            scaled = lambda r: (bt * wr[r:r + 1, :]).astype(BF16)
            lhs = jnp.concatenate([
                jnp.concatenate([scaled(e1), scaled(e2)], axis=1),
                jnp.concatenate([scaled(e_n + e1), scaled(e_n + e2)], axis=1)], axis=0)
            out = jnp.dot(lhs, xblk, preferred_element_type=F32)
            sloc_ref[c, 0, :, lanes] = out[:q]
            sloc_ref[c, 1, :, lanes] = out[q:]
        return carry

    lax.fori_loop(0, nc, prep, 0, unroll=2)

    if has_h0:
        hcur_ref[0] = h0_ref[0].T
        hcur_ref[1] = h0_ref[1].T
    else:
        hcur_ref[...] = jnp.zeros(hcur_ref.shape, F32)

    def carry_states(direction):
        def body(i, carry):
            c = i if direction == 0 else nc - 1 - i
            h = hcur_ref[direction]
            hprev_ref[c, direction] = h.astype(BF16)
            cd = cdb_ref[c]
            r0 = direction * e_n
            decay = jnp.concatenate(
                [jnp.where(first_half[0:1, :], cd[r0 + 2 * j:r0 + 2 * j + 1, :], cd[r0 + 2 * j + 1:r0 + 2 * j + 2, :])
                 for j in range(n_pairs)], axis=1)
            hcur_ref[direction] = h * decay + sloc_ref[c, direction]
            return carry
        lax.fori_loop(0, nc, body, 0)

    carry_states(0)
    carry_states(1)
    if emit_state:
        st_ref[0] = hcur_ref[0].T
        st_ref[1] = hcur_ref[1].T

    dsum = dsk_ref[0:1, :] + dsk_ref[1:2, :]
    sel_k = lax.broadcasted_iota(jnp.int32, (3 * 2 * e_n, 4 * q), 0) % (2 * e_n)
    sel_n = lax.broadcasted_iota(jnp.int32, (3 * 2 * e_n, 4 * q), 1) // q
    spread_sel = [(sel_k == 2 * j + (sel_n % 2) + e_n * (sel_n // 2)).astype(BF16) for j in range(n_pairs)]

    def emit(c, zero_in_f=False, zero_in_b=False):
        rows = chunk_rows(c)
        cc = cs_ref[rows, :]
        cb = jnp.dot(cc, bt_ref[c].astype(BF16), preferred_element_type=F32)
        off_f = None if zero_in_f else jnp.dot(cc, hprev_ref[c, 0], preferred_element_type=F32)
        off_b = None if zero_in_b else jnp.dot(cc, hprev_ref[c, 1], preferred_element_type=F32)
        act3 = act3_ref[rows, :]
        rp = rowp_ref[c]
        rd = rowd_ref[c]
        spreads = [jnp.dot(act3, spread_sel[j], preferred_element_type=F32) for j in range(n_pairs)]
        for j in range(n_pairs):
            lanes = slice(j * 2 * half, (j + 1) * 2 * half)
            spread = spreads[j]
            acf = [spread[:, 0:q], spread[:, q:2 * q]]
            acb = [spread[:, 2 * q:3 * q], spread[:, 3 * q:4 * q]]
            mix = []
            for i, e in enumerate((2 * j, 2 * j + 1)):
                arg = jnp.where(diag, rd[e:e + 1, :],
                                jnp.where(lower, acf[i] - rp[e:e + 1, :], acb[i] - rp[e_n + e:e_n + e + 1, :]))
                mix.append((cb * jnp.exp2(arg)).astype(BF16))
            xp = xs_ref[rows, lanes]
            y = jnp.dot(jnp.concatenate(mix, axis=1), block_diag(xp), preferred_element_type=F32)
            if off_f is not None:
                y = y + off_f[:, lanes] * jnp.exp2(jnp.where(first_half, acf[0], acf[1]))
            if off_b is not None:
                y = y + off_b[:, lanes] * jnp.exp2(jnp.where(first_half, acb[0], acb[1]))
            y_ref[rows, lanes] = y + xp.astype(F32) * dsum[:, lanes]

    if has_h0:
        def emit_body(c, carry):
            emit(c)
            return carry
        lax.fori_loop(0, nc, emit_body, 0, unroll=2)
    else:
        for c in range(nc):
            emit(c, zero_in_f=(c == 0), zero_in_b=(c == nc - 1))


def ssd_scan(proj, dt_hm, conv_w, conv_b, bias_hm, alog_hm, d_skip_wide, *, seq_len, n_seq, row_off, h0=None):
    has_h0 = h0 is not None
    emit_state = not has_h0
    nc = seq_len // SSM_CHUNK
    s_off = row_off // seq_len
    gw = GROUP_WIDTH
    ns = SSM_STATE
    e2 = 2 * HEADS_PER_GROUP
    x_blk0 = SSM_D_INNER // gw
    b_blk0 = 2 * SSM_D_INNER // ns
    c_blk0 = b_blk0 + SSM_GROUPS
    cw_b0 = SSM_D_INNER // ns
    in_specs = [
        pl.BlockSpec((seq_len, gw), lambda s, g: (s_off + s, x_blk0 + g)),
        pl.BlockSpec((seq_len, ns), lambda s, g: (s_off + s, b_blk0 + g)),
        pl.BlockSpec((seq_len, ns), lambda s, g: (s_off + s, c_blk0 + g)),
        pl.BlockSpec((None, 3, gw), lambda s, g: (0, 0, g)),
        pl.BlockSpec((None, 3, ns), lambda s, g: (0, 0, cw_b0 + g)),
        pl.BlockSpec((None, 3, ns), lambda s, g: (0, 0, cw_b0 + SSM_GROUPS + g)),
        pl.BlockSpec((None, 1, gw), lambda s, g: (0, 0, g)),
        pl.BlockSpec((None, 1, ns), lambda s, g: (0, 0, cw_b0 + g)),
        pl.BlockSpec((None, 1, ns), lambda s, g: (0, 0, cw_b0 + SSM_GROUPS + g)),
        pl.BlockSpec((None, nc, e2, SSM_CHUNK), lambda s, g: (g, s_off + s, 0, 0)),
        pl.BlockSpec((None, e2, 1), lambda s, g: (g, 0, 0)),
        pl.BlockSpec((None, e2, 1), lambda s, g: (g, 0, 0)),
        pl.BlockSpec((2, gw), lambda s, g: (0, g)),
    ]
    args = [proj, proj, proj, conv_w, conv_w, conv_w, conv_b, conv_b, conv_b,
            dt_hm, bias_hm, alog_hm, d_skip_wide]
    state_spec = pl.BlockSpec((None, 2, gw, ns), lambda s, g: (s, 0, g, 0))
    if has_h0:
        in_specs.append(state_spec)
        args.append(h0)
    y_spec = pl.BlockSpec((seq_len, gw), lambda s, g: (s, g))
    y_shape = jax.ShapeDtypeStruct((n_seq * seq_len, SSM_D_INNER), F32)
    if emit_state:
        out_specs = [y_spec, state_spec]
        out_shape = [y_shape, jax.ShapeDtypeStruct((n_seq, 2, SSM_D_INNER, ns), F32)]
    else:
        out_specs = y_spec
        out_shape = y_shape
    return pl.pallas_call(
        functools.partial(_ssd_kernel, seq_len=seq_len, has_h0=has_h0, emit_state=emit_state),
        grid=(n_seq, SSM_GROUPS),
        in_specs=in_specs,
        out_specs=out_specs,
        out_shape=out_shape,
        scratch_shapes=[
            pltpu.VMEM((seq_len, gw), BF16),
            pltpu.VMEM((nc, ns, SSM_CHUNK), F32),
            pltpu.VMEM((seq_len, ns), BF16),
            pltpu.VMEM((seq_len, 3 * e2), BF16),
            pltpu.VMEM((nc, e2, SSM_CHUNK), F32),
            pltpu.VMEM((nc, HEADS_PER_GROUP, SSM_CHUNK), F32),
            pltpu.VMEM((nc, e2, SSM_CHUNK), F32),
            pltpu.VMEM((nc, 2, ns, gw), F32),
            pltpu.VMEM((nc, 2, ns, gw), BF16),
            pltpu.VMEM((2, ns, gw), F32)],
        compiler_params=_params(2),
        name="ssd_scan",
    )(*args)


def _gate_norm_kernel(y_ref, z_ref, g_ref, o_ref):
    v = y_ref[...] * _silu(z_ref[...].astype(F32))
    ms = jnp.mean(v * v, axis=-1, keepdims=True)
    o_ref[...] = (v * lax.rsqrt(ms + NORM_EPS) * g_ref[...]).astype(o_ref.dtype)


def gate_norm(y, proj, gain, row_off, tm=256):
    rows = y.shape[0]
    off = row_off // tm
    return pl.pallas_call(
        _gate_norm_kernel,
        grid=(rows // tm,),
        in_specs=[pl.BlockSpec((tm, SSM_D_INNER), lambda m: (m, 0)),
                  pl.BlockSpec((tm, SSM_D_INNER), lambda m: (off + m, 0)),
                  pl.BlockSpec((1, SSM_D_INNER), lambda m: (0, 0))],
        out_specs=pl.BlockSpec((tm, SSM_D_INNER), lambda m: (m, 0)),
        out_shape=jax.ShapeDtypeStruct((rows, SSM_D_INNER), BF16),
        compiler_params=_params(1),
        name="gate_norm",
    )(y, proj, gain.reshape(1, SSM_D_INNER))


def _mlp(x, modt, layer, norm_mlp, mlp_w1, mlp_w2):
    h = norm_mod(x, norm_mlp, modt, layer, 3, 4)
    a = matmul(h, mlp_w1, layer=layer, col_off=0, cols=D_FF, tn=1024, out_dtype=BF16, epilogue="relu2")
    return matmul(a, mlp_w2, layer=layer, col_off=0, cols=D_MODEL, tn=512, tk=D_FF // 2, out_dtype=F32,
                  epilogue="resid", x=x, modt=modt, mod_layer=layer, which=5)


def kernel(x_prompt, x_sample, cache_attn_k, cache_attn_v, state_ssm, c, c_ctx, mod_w, mod_b, norm_mix, norm_mlp, mlp_w1, mlp_w2, na_wqkv, na_wo, na_rpb, ssm_w_in, ssm_conv_w, ssm_conv_b, ssm_dt_bias, ssm_a_log, ssm_d, ssm_norm, ssm_w_out, final_norm):
    x_ctx = x_prompt.reshape(N_CTX, D_MODEL)
    x_lat = x_sample.reshape(N_LAT, D_MODEL)
    cond = jnp.concatenate([c_ctx[None, :], c, jnp.zeros((COND_ROWS - 1 - DEC_BATCH, D_MODEL), F32)], axis=0)
    modt = mod_table(cond, mod_w, mod_b)

    h_ctx = norm_mod(x_ctx, norm_mix, modt, 0, 0, 1)
    h_lat = norm_mod(x_lat, norm_mix, modt, 0, 0, 1, tok_off=N_CTX)
    qkv = functools.partial(matmul, w=na_wqkv, layer=0, cols=D_MODEL, tn=1024)
    q_scale = NA_HEAD_DIM ** -0.5
    q_ctx = qkv(h_ctx, col_off=0, out_dtype=BF16, scale=q_scale)
    k_ctx = qkv(h_ctx, col_off=D_MODEL, out_dtype=F32)
    v_ctx = qkv(h_ctx, col_off=2 * D_MODEL, out_dtype=F32)
    q_lat = qkv(h_lat, col_off=0, out_dtype=BF16, scale=q_scale)
    k_lat = qkv(h_lat, col_off=D_MODEL, out_dtype=BF16)
    v_lat = qkv(h_lat, col_off=2 * D_MODEL, out_dtype=BF16)
    o_ctx = ctx_attention(q_ctx, k_ctx, v_ctx)
    n_attn = cache_attn_k.shape[1]
    o_lat = neighborhood_attention(
        q_lat, k_lat, v_lat,
        cache_attn_k.reshape(DEC_BATCH, n_attn * PAST_LEN, D_MODEL),
        cache_attn_v.reshape(DEC_BATCH, n_attn * PAST_LEN, D_MODEL),
        0, _na_bias_table(na_rpb[0]))
    x = matmul((o_ctx, o_lat), na_wo, layer=0, col_off=0, cols=D_MODEL, tn=512, out_dtype=F32,
               epilogue="resid", x=(x_ctx, x_lat), modt=modt, mod_layer=0, which=2)
    x = _mlp(x, modt, 0, norm_mlp, mlp_w1, mlp_w2)

    h = norm_mod(x, norm_mix, modt, 1, 0, 1)
    zxbc_cols = SSM_D_INNER + SSM_CONV_DIM
    proj = matmul(h, ssm_w_in, layer=0, col_off=0, cols=zxbc_cols, tn=1024, out_dtype=BF16)
    dt_raw = matmul(h, ssm_w_in, layer=0, col_off=zxbc_cols, cols=2 * SSM_HEADS, tn=128, out_dtype=F32)
    e_n = HEADS_PER_GROUP
    dt_hm = dt_raw.reshape(N_TOK // SSM_CHUNK, SSM_CHUNK, 2, SSM_GROUPS, e_n).transpose(3, 0, 2, 4, 1)
    dt_hm = dt_hm.reshape(SSM_GROUPS, N_TOK // SSM_CHUNK, 2 * e_n, SSM_CHUNK)
    by_group = lambda p: p.reshape(2, SSM_GROUPS, e_n).transpose(1, 0, 2).reshape(SSM_GROUPS, 2 * e_n, 1)
    n_ssm = state_ssm.shape[1]
    ssd = functools.partial(
        ssd_scan, proj, dt_hm, ssm_conv_w, ssm_conv_b.reshape(1, 1, SSM_CONV_DIM),
        by_group(ssm_dt_bias[0]), by_group(ssm_a_log[0]), jnp.repeat(ssm_d[0], SSM_HEAD_DIM, axis=1))
    y_ctx, new_state = ssd(seq_len=SEQ, n_seq=BATCH, row_off=0)
    y_lat = ssd(seq_len=DEC_SEQ, n_seq=DEC_BATCH, row_off=N_CTX,
                h0=state_ssm.reshape(DEC_BATCH, n_ssm * 2, SSM_D_INNER, SSM_STATE))
    yn = (gate_norm(y_ctx, proj, ssm_norm, 0), gate_norm(y_lat, proj, ssm_norm, N_CTX))
    x = matmul(yn, ssm_w_out, layer=0, col_off=0, cols=D_MODEL, tn=512, out_dtype=F32,
               epilogue="resid", x=x, modt=modt, mod_layer=1, which=2)
    x = _mlp(x, modt, 1, norm_mlp, mlp_w1, mlp_w2)

    y_prompt = final_rmsnorm(x, final_norm, 0, N_CTX).reshape(BATCH, SEQ, D_MODEL)
    y_sample = final_rmsnorm(x, final_norm, N_CTX, N_LAT).reshape(DEC_BATCH, DEC_SEQ, D_MODEL)
    kv_shape = (BATCH, 1, SEQ, NA_HEADS, NA_HEAD_DIM)
    return (y_prompt, y_sample, k_ctx.reshape(kv_shape), v_ctx.reshape(kv_shape),
            new_state.reshape(BATCH, 1, 2, SSM_HEADS, SSM_HEAD_DIM, SSM_STATE))
```

```python
import functools

import jax
import jax.numpy as jnp
from jax import lax
from jax.experimental import pallas as pl
from jax.experimental.pallas import tpu as pltpu

F32 = jnp.float32
BF16 = jnp.bfloat16

D_MODEL = 2048
BATCH = 32
SEQ = 256
DEPTH = 2
DEC_BATCH = 4
DEC_SEQ = 1024
PAST_LEN = 256
GRID_W = 64
NA_HEADS = 16
NA_HEAD_DIM = 128
WIN_ROWS = 8
WIN_COLS = 16
SSM_D_INNER = 4096
SSM_HEAD_DIM = 64
SSM_HEADS = 64
SSM_GROUPS = 8
SSM_STATE = 128
SSM_CHUNK = 128
SSM_CONV_DIM = SSM_D_INNER + 2 * SSM_GROUPS * SSM_STATE
D_FF = 4 * D_MODEL
NORM_EPS = 1e-6
NEG_INF = -1e30
LOG2E = 1.4426950408889634

N_CTX = BATCH * SEQ
N_LAT = DEC_BATCH * DEC_SEQ
N_TOK = N_CTX + N_LAT
COND_ROWS = 8
HEADS_PER_GROUP = SSM_HEADS // SSM_GROUPS
GROUP_WIDTH = HEADS_PER_GROUP * SSM_HEAD_DIM

VMEM_LIMIT_BYTES = 56 * 1024 * 1024


def _params(n_axes):
    return pltpu.CompilerParams(dimension_semantics=("arbitrary",) * n_axes,
                                vmem_limit_bytes=VMEM_LIMIT_BYTES)


def _silu(v):
    h = 0.5 * v
    return h + h * jnp.tanh(h)


def _softplus(x):
    return jnp.maximum(x, 0.0) + jnp.log(1.0 + jnp.exp(-jnp.abs(x)))


def _cond_row(row_start):
    return jnp.where(row_start < N_CTX, 0, 1 + (row_start - N_CTX) // DEC_SEQ)


def _mod_kernel(cond_ref, w_ref, b_ref, o_ref):
    s = _silu(cond_ref[...]).astype(BF16)
    o_ref[...] = jnp.dot(s, w_ref[...].astype(BF16), preferred_element_type=F32) + b_ref[...]


def mod_table(cond, mod_w, mod_b):
    tn = 1024
    n6 = 6 * D_MODEL
    out = pl.pallas_call(
        _mod_kernel,
        grid=(DEPTH, n6 // tn),
        in_specs=[pl.BlockSpec((COND_ROWS, D_MODEL), lambda l, n: (0, 0)),
                  pl.BlockSpec((None, D_MODEL, tn), lambda l, n: (l, 0, n)),
                  pl.BlockSpec((None, 1, tn), lambda l, n: (l, 0, n))],
        out_specs=pl.BlockSpec((None, COND_ROWS, tn), lambda l, n: (l, 0, n)),
        out_shape=jax.ShapeDtypeStruct((DEPTH, COND_ROWS, n6), F32),
        compiler_params=_params(2),
        name="mod_table",
    )(cond, mod_w, mod_b.reshape(DEPTH, 1, n6))
    return out.reshape(DEPTH, COND_ROWS, 6, 1, D_MODEL)


def _norm_mod_kernel(x_ref, g_ref, shift_ref, scale_ref, o_ref):
    x = x_ref[...]
    ms = jnp.mean(x * x, axis=-1, keepdims=True)
    y = x * lax.rsqrt(ms + NORM_EPS) * g_ref[...]
    o_ref[...] = (y * (1.0 + scale_ref[...]) + shift_ref[...]).astype(o_ref.dtype)


def norm_mod(x, gain, modt, layer, which_shift, which_scale, tok_off=0, tm=512):
    n = x.shape[0]
    mod_spec = lambda which: pl.BlockSpec(
        (None, None, None, 1, D_MODEL), lambda m: (layer, _cond_row(tok_off + m * tm), which, 0, 0))
    return pl.pallas_call(
        _norm_mod_kernel,
        grid=(n // tm,),
        in_specs=[pl.BlockSpec((tm, D_MODEL), lambda m: (m, 0)),
                  pl.BlockSpec((None, 1, D_MODEL), lambda m: (layer, 0, 0)),
                  mod_spec(which_shift), mod_spec(which_scale)],
        out_specs=pl.BlockSpec((tm, D_MODEL), lambda m: (m, 0)),
        out_shape=jax.ShapeDtypeStruct((n, D_MODEL), BF16),
        compiler_params=_params(1),
        name="norm_mod",
    )(x, gain.reshape(-1, 1, D_MODEL), modt, modt)


def _final_norm_kernel(x_ref, g_ref, o_ref):
    x = x_ref[...]
    ms = jnp.mean(x * x, axis=-1, keepdims=True)
    o_ref[...] = x * lax.rsqrt(ms + NORM_EPS) * g_ref[...]


def final_rmsnorm(x, gain, row_off, rows, tm=512):
    off = row_off // tm
    return pl.pallas_call(
        _final_norm_kernel,
        grid=(rows // tm,),
        in_specs=[pl.BlockSpec((tm, D_MODEL), lambda m: (off + m, 0)),
                  pl.BlockSpec((1, D_MODEL), lambda m: (0, 0))],
        out_specs=pl.BlockSpec((tm, D_MODEL), lambda m: (m, 0)),
        out_shape=jax.ShapeDtypeStruct((rows, D_MODEL), F32),
        compiler_params=_params(1),
        name="final_norm",
    )(x, gain.reshape(1, D_MODEL))


def _mm_kernel(*refs, nk, epilogue, scale, n_a, n_x, split):
    refs = list(refs)
    a_refs = [refs.pop(0) for _ in range(n_a)]
    w_ref = refs.pop(0)
    x_refs = [refs.pop(0) for _ in range(n_x)]
    gate_ref = refs.pop(0) if epilogue == "resid" else None
    o_ref = refs.pop(0)
    wb_ref = refs.pop(0)
    acc_ref = refs.pop(0) if nk > 1 else None
    m = pl.program_id(1)
    k = pl.program_id(2)

    def pick(parts):
        if len(parts) == 1:
            return parts[0][...]
        return jnp.where(m < split, parts[0][...], parts[1][...])

    @pl.when(m == 0)
    def _():
        wb_ref[k] = w_ref[...].astype(BF16)

    part = jnp.dot(pick(a_refs), wb_ref[k], preferred_element_type=F32)

    def finish(acc):
        if epilogue == "cast":
            if scale is not None:
                acc = acc * scale
            o_ref[...] = acc.astype(o_ref.dtype)
        elif epilogue == "relu2":
            r = jnp.maximum(acc, 0.0)
            o_ref[...] = (r * r).astype(o_ref.dtype)
        else:
            o_ref[...] = pick(x_refs) + gate_ref[...] * acc

    if nk == 1:
        finish(part)
    else:
        @pl.when(k == 0)
        def _():
            acc_ref[...] = part

        @pl.when(jnp.logical_and(k > 0, k < nk - 1))
        def _():
            acc_ref[...] += part

        @pl.when(k == nk - 1)
        def _():
            finish(acc_ref[...] + part)


def matmul(a, w, *, layer, col_off, cols, tn, out_dtype, tm=1024, tk=2048, row_off=0, rows=None,
           epilogue="cast", scale=None, x=None, modt=None, mod_layer=None, which=None):
    a_parts = a if isinstance(a, tuple) else (a,)
    x_parts = () if x is None else (x if isinstance(x, tuple) else (x,))
    kdim = a_parts[0].shape[1]
    if rows is None:
        rows = sum(p.shape[0] for p in a_parts) - row_off
    nk, nn, nm = kdim // tk, cols // tn, rows // tm
    m_off, n_off = row_off // tm, col_off // tn
    split = N_CTX // tm

    def row_specs(parts, width, col_index):
        if len(parts) == 1:
            return [pl.BlockSpec((tm, width), lambda n, m, k: (m_off + m, col_index(n, m, k, True)))]
        return [pl.BlockSpec((tm, width), lambda n, m, k: (jnp.minimum(m, split - 1), col_index(n, m, k, m < split))),
                pl.BlockSpec((tm, width), lambda n, m, k: (jnp.maximum(m - split, 0), col_index(n, m, k, m >= split)))]

    in_specs = row_specs(a_parts, tk, lambda n, m, k, live: jnp.where(live, k, 0))
    in_specs.append(
        pl.BlockSpec((None, tk, tn), lambda n, m, k: (layer, jnp.where(m == 0, k, nk - 1), n_off + n)))
    args = list(a_parts) + [w]
    if epilogue == "resid":
        in_specs += row_specs(x_parts, tn, lambda n, m, k, live: n)
        in_specs.append(pl.BlockSpec((None, None, None, 1, tn),
                                     lambda n, m, k: (mod_layer, _cond_row((m_off + m) * tm), which, 0, n)))
        args += list(x_parts) + [modt]
    scratch = [pltpu.VMEM((nk, tk, tn), BF16)]
    if nk > 1:
        scratch.append(pltpu.VMEM((tm, tn), F32))
    return pl.pallas_call(
        functools.partial(_mm_kernel, nk=nk, epilogue=epilogue, scale=scale,
                          n_a=len(a_parts), n_x=len(x_parts), split=split),
        grid=(nn, nm, nk),
        in_specs=in_specs,
        out_specs=pl.BlockSpec((tm, tn), lambda n, m, k: (m, n)),
        out_shape=jax.ShapeDtypeStruct((rows, cols), out_dtype),
        scratch_shapes=scratch,
        compiler_params=_params(3),
        name="matmul_" + epilogue,
    )(*args)


def _ctx_attn_kernel(q_ref, k_ref, v_ref, o_ref):
    for h in range(NA_HEADS):
        sl = slice(h * NA_HEAD_DIM, (h + 1) * NA_HEAD_DIM)
        q = q_ref[:, sl]
        k = k_ref[:, sl].astype(BF16)
        v = v_ref[:, sl].astype(BF16)
        s = lax.dot_general(q, k, (((1,), (1,)), ((), ())), preferred_element_type=F32)
        p = jnp.exp(s - jnp.max(s, axis=-1, keepdims=True))
        l = jnp.sum(p, axis=-1, keepdims=True)
        o = jnp.dot(p.astype(BF16), v, preferred_element_type=F32) / l
        o_ref[:, sl] = o.astype(o_ref.dtype)


def ctx_attention(q, k, v):
    spec = pl.BlockSpec((SEQ, D_MODEL), lambda b: (b, 0))
    return pl.pallas_call(
        _ctx_attn_kernel,
        grid=(BATCH,),
        in_specs=[spec, spec, spec],
        out_specs=spec,
        out_shape=jax.ShapeDtypeStruct((N_CTX, D_MODEL), BF16),
        compiler_params=_params(1),
        name="ctx_attention",
    )(q, k, v)


GRID_ROWS = DEC_SEQ // GRID_W
Q_BLOCK_ROWS = 4
KEY_WIN_ROWS = 12
N_ROW_OFFS = 2 * WIN_ROWS - 1
TILE_BOTH, TILE_SECOND, TILE_FIRST = 0, N_ROW_OFFS - 1, 2 * N_ROW_OFFS - 1
TILE_NONE = 3 * N_ROW_OFFS - 1
N_BIAS_TILES = TILE_NONE + 1


def _row_start(r):
    return min(max(r - WIN_ROWS // 2, 0), GRID_ROWS - WIN_ROWS)


def _key_win_start(rb):
    return min(max(rb * Q_BLOCK_ROWS - WIN_ROWS // 2, 0), GRID_ROWS - KEY_WIN_ROWS)


def _na_bias_table(rpb):
    reach = WIN_COLS - 1
    by_offset = jnp.concatenate([jnp.repeat(rpb[..., :1], GRID_W - 1 - reach, axis=-1), rpb,
                                 jnp.repeat(rpb[..., -1:], GRID_W - 1 - reach, axis=-1)], axis=-1)
    t = jnp.stack([by_offset[..., GRID_W - 1 - qc:2 * GRID_W - 1 - qc] for qc in range(GRID_W)], axis=-2)
    qc = jnp.arange(GRID_W)[:, None]
    kc = jnp.arange(GRID_W)[None, :]
    ws = jnp.clip(qc - WIN_COLS // 2, 0, GRID_W - WIN_COLS)
    t = jnp.where((kc >= ws) & (kc < ws + WIN_COLS), t, NEG_INF)
    masked = jnp.full_like(t, NEG_INF)
    return jnp.concatenate([
        jnp.concatenate([t[:, :-1], t[:, 1:]], axis=-1),
        jnp.concatenate([masked, t], axis=-1),
        jnp.concatenate([t, masked], axis=-1),
        jnp.concatenate([masked[:, :1], masked[:, :1]], axis=-1)], axis=1)


def _bias_tile_index(r, key_row):
    rs = _row_start(r)
    first_ok = rs <= key_row < rs + WIN_ROWS
    second_ok = rs <= key_row + 1 < rs + WIN_ROWS
    off = key_row - r + WIN_ROWS - 1
    if first_ok and second_ok:
        return TILE_BOTH + off
    if second_ok:
        return TILE_SECOND + off + 1
    if first_ok:
        return TILE_FIRST + off
    return TILE_NONE


def _na_kernel(q_ref, k_ref, v_ref, ck_ref, cv_ref, t_ref, o_ref):
    ck = ck_ref[...].astype(BF16)
    cv = cv_ref[...].astype(BF16)
    nt = (((1,), (1,)), ((), ()))
    blk = Q_BLOCK_ROWS * GRID_W
    for rb in range(GRID_ROWS // Q_BLOCK_ROWS):
        kw0 = _key_win_start(rb)
        keys = slice(kw0 * GRID_W, (kw0 + KEY_WIN_ROWS) * GRID_W)
        q = q_ref[rb * blk:(rb + 1) * blk, :]
        bias = jnp.concatenate([
            jnp.concatenate([t_ref[_bias_tile_index(rb * Q_BLOCK_ROWS + i, kw0 + 2 * p)]
                             for p in range(KEY_WIN_ROWS // 2)], axis=1)
            for i in range(Q_BLOCK_ROWS)], axis=0)
        s_loc = lax.dot_general(q, k_ref[keys, :], nt, preferred_element_type=F32) + bias
        s_ctx = lax.dot_general(q, ck, nt, preferred_element_type=F32)
        mx = jnp.maximum(jnp.max(s_loc, axis=-1, keepdims=True), jnp.max(s_ctx, axis=-1, keepdims=True))
        p_loc = jnp.exp(s_loc - mx)
        p_ctx = jnp.exp(s_ctx - mx)
        l = jnp.sum(p_loc, axis=-1, keepdims=True) + jnp.sum(p_ctx, axis=-1, keepdims=True)
        o = (jnp.dot(p_loc.astype(BF16), v_ref[keys, :], preferred_element_type=F32)
             + jnp.dot(p_ctx.astype(BF16), cv, preferred_element_type=F32)) / l
        o_ref[rb * blk:(rb + 1) * blk, :] = o.astype(o_ref.dtype)


def neighborhood_attention(q, k, v, cache_k, cache_v, cache_layer, bias_table):
    tok = pl.BlockSpec((DEC_SEQ, NA_HEAD_DIM), lambda h, b: (b, h))
    cache = pl.BlockSpec((None, PAST_LEN, NA_HEAD_DIM), lambda h, b: (b, cache_layer, h))
    return pl.pallas_call(
        _na_kernel,
        grid=(NA_HEADS, DEC_BATCH),
        in_specs=[tok, tok, tok, cache, cache,
                  pl.BlockSpec((None, N_BIAS_TILES, GRID_W, 2 * GRID_W), lambda h, b: (h, 0, 0, 0))],
        out_specs=tok,
        out_shape=jax.ShapeDtypeStruct((N_LAT, D_MODEL), BF16),
        compiler_params=_params(2),
        name="neighborhood_attention",
    )(q, k, v, cache_k, cache_v, bias_table)


N_SSD_INPUTS = 13


def _ssd_kernel(*refs, seq_len, has_h0, emit_state, seqs_per_step):
    n_in = N_SSD_INPUTS + (1 if has_h0 else 0)
    n_out = 1 + (1 if emit_state else 0)
    ins, outs, scratch = refs[:n_in], refs[n_in:n_in + n_out], refs[n_in + n_out:]
    nc = seq_len // SSM_CHUNK
    for s in range(seqs_per_step):
        tok = pl.ds(s * seq_len, seq_len)
        views = [ins[0].at[tok], ins[1].at[tok], ins[2].at[tok]] + list(ins[3:9])
        views += [ins[9].at[pl.ds(s * nc, nc)]] + list(ins[10:N_SSD_INPUTS])
        if has_h0:
            views.append(ins[N_SSD_INPUTS].at[s])
        views.append(outs[0].at[tok])
        if emit_state:
            views.append(outs[1].at[s])
        views += [r.at[s] for r in scratch]
        _ssd_sequence(*views, seq_len=seq_len, has_h0=has_h0, emit_state=emit_state)


def _ssd_sequence(*refs, seq_len, has_h0, emit_state):
    (x_ref, b_ref, c_ref, cwx_ref, cwb_ref, cwc_ref, cbx_ref, cbb_ref, cbc_ref,
     dt_ref, bias_ref, alog_ref, dsk_ref) = refs[:N_SSD_INPUTS]
    rest = list(refs[N_SSD_INPUTS:])
    h0_ref = rest.pop(0) if has_h0 else None
    y_ref = rest.pop(0)
    st_ref = rest.pop(0) if emit_state else None
    (xs_ref, bt_ref, cs_ref, act3_ref, rowp_ref, rowd_ref, cdb_ref, wr_ref, sloc_ref, hprev_ref, hcur_ref) = rest

    q = SSM_CHUNK
    nc = seq_len // q
    hp = lax.Precision.HIGHEST
    e_n = HEADS_PER_GROUP
    half = SSM_HEAD_DIM
    n_pairs = e_n // 2

    li = lax.broadcasted_iota(jnp.int32, (q, q), 0)
    si = lax.broadcasted_iota(jnp.int32, (q, q), 1)
    tril = (si <= li).astype(F32)
    triu = (si >= li).astype(F32)
    lower = si <= li
    diag = si == li
    first_half = si < half
    fwd_row = lax.broadcasted_iota(jnp.int32, (2 * e_n, q), 0) < e_n
    a_hm = -jnp.exp(alog_ref[...]) * LOG2E

    def chunk_rows(c):
        return pl.ds(c * q, q) if isinstance(c, int) else pl.ds(pl.multiple_of(c * q, q), q)

    def block_diag(pair):
        zero = jnp.zeros_like(pair)
        return jnp.concatenate([jnp.where(first_half, pair, zero), jnp.where(first_half, zero, pair)], axis=0)

    dt_all = _softplus(dt_ref[...] + bias_ref[...])
    da_all = (dt_all * a_hm).reshape(nc * 2 * e_n, q)
    fwd_rows = (lax.broadcasted_iota(jnp.int32, (nc * 2 * e_n, q), 0) % (2 * e_n)) < e_n
    ac_all = jnp.where(fwd_rows,
                       jnp.dot(da_all, triu, precision=hp, preferred_element_type=F32),
                       jnp.dot(da_all, tril, precision=hp, preferred_element_type=F32)).reshape(nc, 2 * e_n, q)
    rowp_ref[...] = ac_all - jnp.log2(dt_all)
    rowd_ref[...] = jnp.log2(dt_all[:, :e_n, :] + dt_all[:, e_n:, :])
    ends = jnp.broadcast_to(jnp.where(fwd_row[None, :, 0:1], ac_all[:, :, q - 1:q], ac_all[:, :, 0:1]), ac_all.shape)
    wr_ref[...] = jnp.exp2(ends - ac_all) * dt_all
    cdb_ref[...] = jnp.exp2(ends)
    hi = ac_all.astype(BF16).astype(F32)
    mid = (ac_all - hi).astype(BF16).astype(F32)
    lo = (ac_all - hi - mid).astype(BF16).astype(F32)
    for c in range(nc):
        act3_ref[c * q:(c + 1) * q, :] = jnp.concatenate([hi[c], mid[c], lo[c]], axis=0).T.astype(BF16)

    def prep(c, carry):
        rows = chunk_rows(c)
        r0 = c * q

        def conv_silu(u_ref, w_ref, bias_r):
            u = u_ref[rows, :].astype(F32)
            width = u.shape[1]
            before = u_ref[pl.ds(pl.multiple_of(jnp.maximum(r0 - 16, 0), 16), 16), :][15:16, :].astype(F32)
            after = u_ref[pl.ds(pl.multiple_of(jnp.minimum(r0 + q, seq_len - 16), 16), 16), :][0:1, :].astype(F32)
            before = jnp.where(c > 0, before, 0.0)
            after = jnp.where(c < nc - 1, after, 0.0)
            rid = lax.broadcasted_iota(jnp.int32, (q, width), 0)
            prev = jnp.where(rid == 0, before, pltpu.roll(u, 1, axis=0))
            nxt = jnp.where(rid == q - 1, after, pltpu.roll(u, q - 1, axis=0))
            w = w_ref[...] * 0.5
            h = prev * w[0:1] + u * w[1:2] + nxt * w[2:3] + bias_r[...] * 0.5
            return h + h * jnp.tanh(h)

        xc = conv_silu(x_ref, cwx_ref, cbx_ref).astype(BF16)
        xs_ref[rows, :] = xc
        cs_ref[rows, :] = conv_silu(c_ref, cwc_ref, cbc_ref).astype(BF16)
        bt = conv_silu(b_ref, cwb_ref, cbb_ref).T
        bt_ref[c] = bt
        wr = wr_ref[c]

        for j in range(n_pairs):
            lanes = slice(j * 2 * half, (j + 1) * 2 * half)
            xblk = block_diag(xc[:, lanes])
            e1, e2 = 2 * j, 2 * j + 1
            scaled = lambda r: (bt * wr[r:r + 1, :]).astype(BF16)
            lhs = jnp.concatenate([
                jnp.concatenate([scaled(e1), scaled(e2)], axis=1),
                jnp.concatenate([scaled(e_n + e1), scaled(e_n + e2)], axis=1)], axis=0)
            out = jnp.dot(lhs, xblk, preferred_element_type=F32)
            sloc_ref[c, 0, :, lanes] = out[:q]
            sloc_ref[c, 1, :, lanes] = out[q:]
        return carry

    unrolled = nc <= 2
    if unrolled:
        for c in range(nc):
            prep(c, 0)
    else:
        lax.fori_loop(0, nc, prep, 0, unroll=2)

    if has_h0:
        hcur_ref[0] = h0_ref[0].T
        hcur_ref[1] = h0_ref[1].T
    else:
        hcur_ref[...] = jnp.zeros(hcur_ref.shape, F32)

    def carry_states(direction):
        def body(i, carry):
            c = i if direction == 0 else nc - 1 - i
            h = hcur_ref[direction]
            hprev_ref[c, direction] = h.astype(BF16)
            cd = cdb_ref[c]
            r0 = direction * e_n
            decay = jnp.concatenate(
                [jnp.where(first_half[0:1, :], cd[r0 + 2 * j:r0 + 2 * j + 1, :], cd[r0 + 2 * j + 1:r0 + 2 * j + 2, :])
                 for j in range(n_pairs)], axis=1)
            hcur_ref[direction] = h * decay + sloc_ref[c, direction]
            return carry
        if unrolled:
            for i in range(nc):
                body(i, 0)
        else:
            lax.fori_loop(0, nc, body, 0)

    carry_states(0)
    carry_states(1)
    if emit_state:
        st_ref[0] = hcur_ref[0].T
        st_ref[1] = hcur_ref[1].T

    dsum = dsk_ref[0:1, :] + dsk_ref[1:2, :]
    sel_k = lax.broadcasted_iota(jnp.int32, (3 * 2 * e_n, 4 * q), 0) % (2 * e_n)
    sel_n = lax.broadcasted_iota(jnp.int32, (3 * 2 * e_n, 4 * q), 1) // q
    spread_sel = [(sel_k == 2 * j + (sel_n % 2) + e_n * (sel_n // 2)).astype(BF16) for j in range(n_pairs)]

    def emit(c, zero_in_f=False, zero_in_b=False):
        rows = chunk_rows(c)
        cc = cs_ref[rows, :]
        cb = jnp.dot(cc, bt_ref[c].astype(BF16), preferred_element_type=F32)
        off_f = None if zero_in_f else jnp.dot(cc, hprev_ref[c, 0], preferred_element_type=F32)
        off_b = None if zero_in_b else jnp.dot(cc, hprev_ref[c, 1], preferred_element_type=F32)
        act3 = act3_ref[rows, :]
        rp = rowp_ref[c]
        rd = rowd_ref[c]
        spreads = [jnp.dot(act3, spread_sel[j], preferred_element_type=F32) for j in range(n_pairs)]
        for j in range(n_pairs):
            lanes = slice(j * 2 * half, (j + 1) * 2 * half)
            spread = spreads[j]
            acf = [spread[:, 0:q], spread[:, q:2 * q]]
            acb = [spread[:, 2 * q:3 * q], spread[:, 3 * q:4 * q]]
            mix = []
            for i, e in enumerate((2 * j, 2 * j + 1)):
                arg = jnp.where(diag, rd[e:e + 1, :],
                                jnp.where(lower, acf[i] - rp[e:e + 1, :], acb[i] - rp[e_n + e:e_n + e + 1, :]))
                mix.append((cb * jnp.exp2(arg)).astype(BF16))
            xp = xs_ref[rows, lanes]
            y = jnp.dot(jnp.concatenate(mix, axis=1), block_diag(xp), preferred_element_type=F32)
            if off_f is not None:
                y = y + off_f[:, lanes] * jnp.exp2(jnp.where(first_half, acf[0], acf[1]))
            if off_b is not None:
                y = y + off_b[:, lanes] * jnp.exp2(jnp.where(first_half, acb[0], acb[1]))
            y_ref[rows, lanes] = (y + xp.astype(F32) * dsum[:, lanes]).astype(y_ref.dtype)

    if has_h0:
        def emit_body(c, carry):
            emit(c)
            return carry
        lax.fori_loop(0, nc, emit_body, 0, unroll=2)
    else:
        for c in range(nc):
            emit(c, zero_in_f=(c == 0), zero_in_b=(c == nc - 1))


def ssd_scan(proj, dt_hm, conv_w, conv_b, bias_hm, alog_hm, d_skip_wide, *, seq_len, n_seq, row_off,
             seqs_per_step=1, h0=None):
    has_h0 = h0 is not None
    emit_state = not has_h0
    nc = seq_len // SSM_CHUNK
    sps = seqs_per_step
    s_off = row_off // (sps * seq_len)
    gw = GROUP_WIDTH
    ns = SSM_STATE
    e2 = 2 * HEADS_PER_GROUP
    x_blk0 = SSM_D_INNER // gw
    b_blk0 = 2 * SSM_D_INNER // ns
    c_blk0 = b_blk0 + SSM_GROUPS
    cw_b0 = SSM_D_INNER // ns
    in_specs = [
        pl.BlockSpec((sps * seq_len, gw), lambda s, g: (s_off + s, x_blk0 + g)),
        pl.BlockSpec((sps * seq_len, ns), lambda s, g: (s_off + s, b_blk0 + g)),
        pl.BlockSpec((sps * seq_len, ns), lambda s, g: (s_off + s, c_blk0 + g)),
        pl.BlockSpec((None, 3, gw), lambda s, g: (0, 0, g)),
        pl.BlockSpec((None, 3, ns), lambda s, g: (0, 0, cw_b0 + g)),
        pl.BlockSpec((None, 3, ns), lambda s, g: (0, 0, cw_b0 + SSM_GROUPS + g)),
        pl.BlockSpec((None, 1, gw), lambda s, g: (0, 0, g)),
        pl.BlockSpec((None, 1, ns), lambda s, g: (0, 0, cw_b0 + g)),
        pl.BlockSpec((None, 1, ns), lambda s, g: (0, 0, cw_b0 + SSM_GROUPS + g)),
        pl.BlockSpec((None, sps * nc, e2, SSM_CHUNK), lambda s, g: (g, s_off + s, 0, 0)),
        pl.BlockSpec((None, e2, 1), lambda s, g: (g, 0, 0)),
        pl.BlockSpec((None, e2, 1), lambda s, g: (g, 0, 0)),
        pl.BlockSpec((2, gw), lambda s, g: (0, g)),
    ]
    args = [proj, proj, proj, conv_w, conv_w, conv_w, conv_b, conv_b, conv_b,
            dt_hm, bias_hm, alog_hm, d_skip_wide]
    assert len(args) == N_SSD_INPUTS
    state_spec = pl.BlockSpec((sps, 2, gw, ns), lambda s, g: (s, 0, g, 0))
    if has_h0:
        in_specs.append(state_spec)
        args.append(h0)
    y_spec = pl.BlockSpec((sps * seq_len, gw), lambda s, g: (s, g))
    y_shape = jax.ShapeDtypeStruct((n_seq * seq_len, SSM_D_INNER), BF16)
    if emit_state:
        out_specs = [y_spec, state_spec]
        out_shape = [y_shape, jax.ShapeDtypeStruct((n_seq, 2, SSM_D_INNER, ns), F32)]
    else:
        out_specs = y_spec
        out_shape = y_shape
    return pl.pallas_call(
        functools.partial(_ssd_kernel, seq_len=seq_len, has_h0=has_h0, emit_state=emit_state,
                          seqs_per_step=sps),
        grid=(n_seq // sps, SSM_GROUPS),
        in_specs=in_specs,
        out_specs=out_specs,
        out_shape=out_shape,
        scratch_shapes=[
            pltpu.VMEM((sps, seq_len, gw), BF16),
            pltpu.VMEM((sps, nc, ns, SSM_CHUNK), F32),
            pltpu.VMEM((sps, seq_len, ns), BF16),
            pltpu.VMEM((sps, seq_len, 3 * e2), BF16),
            pltpu.VMEM((sps, nc, e2, SSM_CHUNK), F32),
            pltpu.VMEM((sps, nc, HEADS_PER_GROUP, SSM_CHUNK), F32),
            pltpu.VMEM((sps, nc, e2, SSM_CHUNK), F32),
            pltpu.VMEM((sps, nc, e2, SSM_CHUNK), F32),
            pltpu.VMEM((sps, nc, 2, ns, gw), F32),
            pltpu.VMEM((sps, nc, 2, ns, gw), BF16),
            pltpu.VMEM((sps, 2, ns, gw), F32)],
        compiler_params=_params(2),
        name="ssd_scan",
    )(*args)


def _gate_norm_kernel(y_ref, z_ref, g_ref, o_ref):
    v = y_ref[...].astype(F32) * _silu(z_ref[...].astype(F32))
    ms = jnp.mean(v * v, axis=-1, keepdims=True)
    o_ref[...] = (v * lax.rsqrt(ms + NORM_EPS) * g_ref[...]).astype(o_ref.dtype)


def gate_norm(y, proj, gain, row_off, tm=256):
    rows = y.shape[0]
    off = row_off // tm
    return pl.pallas_call(
        _gate_norm_kernel,
        grid=(rows // tm,),
        in_specs=[pl.BlockSpec((tm, SSM_D_INNER), lambda m: (m, 0)),
                  pl.BlockSpec((tm, SSM_D_INNER), lambda m: (off + m, 0)),
                  pl.BlockSpec((1, SSM_D_INNER), lambda m: (0, 0))],
        out_specs=pl.BlockSpec((tm, SSM_D_INNER), lambda m: (m, 0)),
        out_shape=jax.ShapeDtypeStruct((rows, SSM_D_INNER), BF16),
        compiler_params=_params(1),
        name="gate_norm",
    )(y, proj, gain.reshape(1, SSM_D_INNER))


def _mlp(x, modt, layer, norm_mlp, mlp_w1, mlp_w2):
    h = norm_mod(x, norm_mlp, modt, layer, 3, 4)
    a = matmul(h, mlp_w1, layer=layer, col_off=0, cols=D_FF, tn=1024, tm=2048, out_dtype=BF16, epilogue="relu2")
    return matmul(a, mlp_w2, layer=layer, col_off=0, cols=D_MODEL, tn=512, tk=D_FF // 2, out_dtype=F32,
                  epilogue="resid", x=x, modt=modt, mod_layer=layer, which=5)


def kernel(x_prompt, x_sample, cache_attn_k, cache_attn_v, state_ssm, c, c_ctx, mod_w, mod_b, norm_mix, norm_mlp, mlp_w1, mlp_w2, na_wqkv, na_wo, na_rpb, ssm_w_in, ssm_conv_w, ssm_conv_b, ssm_dt_bias, ssm_a_log, ssm_d, ssm_norm, ssm_w_out, final_norm):
    x_ctx = x_prompt.reshape(N_CTX, D_MODEL)
    x_lat = x_sample.reshape(N_LAT, D_MODEL)
    cond = jnp.concatenate([c_ctx[None, :], c, jnp.zeros((COND_ROWS - 1 - DEC_BATCH, D_MODEL), F32)], axis=0)
    modt = mod_table(cond, mod_w, mod_b)

    h_ctx = norm_mod(x_ctx, norm_mix, modt, 0, 0, 1)
    h_lat = norm_mod(x_lat, norm_mix, modt, 0, 0, 1, tok_off=N_CTX)
    qkv = functools.partial(matmul, w=na_wqkv, layer=0, cols=D_MODEL, tn=1024)
    q_scale = NA_HEAD_DIM ** -0.5
    q_ctx = qkv(h_ctx, col_off=0, out_dtype=BF16, scale=q_scale)
    k_ctx = qkv(h_ctx, col_off=D_MODEL, out_dtype=F32)
    v_ctx = qkv(h_ctx, col_off=2 * D_MODEL, out_dtype=F32)
    q_lat = qkv(h_lat, col_off=0, out_dtype=BF16, scale=q_scale)
    k_lat = qkv(h_lat, col_off=D_MODEL, out_dtype=BF16)
    v_lat = qkv(h_lat, col_off=2 * D_MODEL, out_dtype=BF16)
    o_ctx = ctx_attention(q_ctx, k_ctx, v_ctx)
    n_attn = cache_attn_k.shape[1]
    o_lat = neighborhood_attention(
        q_lat, k_lat, v_lat,
        cache_attn_k.reshape(DEC_BATCH, n_attn * PAST_LEN, D_MODEL),
        cache_attn_v.reshape(DEC_BATCH, n_attn * PAST_LEN, D_MODEL),
        0, _na_bias_table(na_rpb[0]))
    x = matmul((o_ctx, o_lat), na_wo, layer=0, col_off=0, cols=D_MODEL, tn=1024, tm=512, out_dtype=F32,
               epilogue="resid", x=(x_ctx, x_lat), modt=modt, mod_layer=0, which=2)
    x = _mlp(x, modt, 0, norm_mlp, mlp_w1, mlp_w2)

    h = norm_mod(x, norm_mix, modt, 1, 0, 1)
    zxbc_cols = SSM_D_INNER + SSM_CONV_DIM
    proj = matmul(h, ssm_w_in, layer=0, col_off=0, cols=zxbc_cols, tn=1024, tm=2048, out_dtype=BF16)
    dt_raw = matmul(h, ssm_w_in, layer=0, col_off=zxbc_cols, cols=2 * SSM_HEADS, tn=128, out_dtype=F32)
    e_n = HEADS_PER_GROUP
    dt_hm = dt_raw.reshape(N_TOK // SSM_CHUNK, SSM_CHUNK, 2, SSM_GROUPS, e_n).transpose(3, 0, 2, 4, 1)
    dt_hm = dt_hm.reshape(SSM_GROUPS, N_TOK // SSM_CHUNK, 2 * e_n, SSM_CHUNK)
    by_group = lambda p: p.reshape(2, SSM_GROUPS, e_n).transpose(1, 0, 2).reshape(SSM_GROUPS, 2 * e_n, 1)
    n_ssm = state_ssm.shape[1]
    ssd = functools.partial(
        ssd_scan, proj, dt_hm, ssm_conv_w, ssm_conv_b.reshape(1, 1, SSM_CONV_DIM),
        by_group(ssm_dt_bias[0]), by_group(ssm_a_log[0]), jnp.repeat(ssm_d[0], SSM_HEAD_DIM, axis=1))
    y_ctx, new_state = ssd(seq_len=SEQ, n_seq=BATCH, row_off=0, seqs_per_step=4)
    y_lat = ssd(seq_len=DEC_SEQ, n_seq=DEC_BATCH, row_off=N_CTX,
                h0=state_ssm.reshape(DEC_BATCH, n_ssm * 2, SSM_D_INNER, SSM_STATE))
    yn = (gate_norm(y_ctx, proj, ssm_norm, 0), gate_norm(y_lat, proj, ssm_norm, N_CTX))
    x = matmul(yn, ssm_w_out, layer=0, col_off=0, cols=D_MODEL, tn=512, tm=512, tk=SSM_D_INNER, out_dtype=F32,
               epilogue="resid", x=x, modt=modt, mod_layer=1, which=2)
    x = _mlp(x, modt, 1, norm_mlp, mlp_w1, mlp_w2)

    y_prompt = final_rmsnorm(x, final_norm, 0, N_CTX).reshape(BATCH, SEQ, D_MODEL)
    y_sample = final_rmsnorm(x, final_norm, N_CTX, N_LAT).reshape(DEC_BATCH, DEC_SEQ, D_MODEL)
    kv_shape = (BATCH, 1, SEQ, NA_HEADS, NA_HEAD_DIM)
    return (y_prompt, y_sample, k_ctx.reshape(kv_shape), v_ctx.reshape(kv_shape),
            new_state.reshape(BATCH, 1, 2, SSM_HEADS, SSM_HEAD_DIM, SSM_STATE))
```

```python
import functools

import jax
import jax.numpy as jnp
import numpy as np
from jax import lax
from jax.experimental import pallas as pl
from jax.experimental.pallas import tpu as pltpu

F32 = jnp.float32
BF16 = jnp.bfloat16

D_MODEL = 2048
BATCH = 32
SEQ = 256
DEPTH = 2
DEC_BATCH = 4
DEC_SEQ = 1024
PAST_LEN = 256
GRID_W = 64
NA_HEADS = 16
NA_HEAD_DIM = 128
WIN_ROWS = 8
WIN_COLS = 16
SSM_D_INNER = 4096
SSM_HEAD_DIM = 64
SSM_HEADS = 64
SSM_GROUPS = 8
SSM_STATE = 128
SSM_CHUNK = 128
SSM_CONV_DIM = SSM_D_INNER + 2 * SSM_GROUPS * SSM_STATE
D_FF = 4 * D_MODEL
NORM_EPS = 1e-6
NEG_INF = -1e30
LOG2E = 1.4426950408889634

N_CTX = BATCH * SEQ
N_LAT = DEC_BATCH * DEC_SEQ
N_TOK = N_CTX + N_LAT
COND_ROWS = 8
HEADS_PER_GROUP = SSM_HEADS // SSM_GROUPS
GROUP_WIDTH = HEADS_PER_GROUP * SSM_HEAD_DIM

VMEM_LIMIT_BYTES = 56 * 1024 * 1024


def _params(n_axes):
    return pltpu.CompilerParams(dimension_semantics=("arbitrary",) * n_axes,
                                vmem_limit_bytes=VMEM_LIMIT_BYTES)


def _silu(v):
    h = 0.5 * v
    return h + h * jnp.tanh(h)


def _softplus(x):
    return jnp.maximum(x, 0.0) + jnp.log(1.0 + jnp.exp(-jnp.abs(x)))


def _cond_row(row_start):
    return jnp.where(row_start < N_CTX, 0, 1 + (row_start - N_CTX) // DEC_SEQ)


def _mod_kernel(cond_ref, w_ref, b_ref, o_ref):
    s = _silu(cond_ref[...]).astype(BF16)
    o_ref[...] = jnp.dot(s, w_ref[...].astype(BF16), preferred_element_type=F32) + b_ref[...]


def mod_table(cond, mod_w, mod_b):
    tn = 1024
    n6 = 6 * D_MODEL
    out = pl.pallas_call(
        _mod_kernel,
        grid=(DEPTH, n6 // tn),
        in_specs=[pl.BlockSpec((COND_ROWS, D_MODEL), lambda l, n: (0, 0)),
                  pl.BlockSpec((None, D_MODEL, tn), lambda l, n: (l, 0, n)),
                  pl.BlockSpec((None, 1, tn), lambda l, n: (l, 0, n))],
        out_specs=pl.BlockSpec((None, COND_ROWS, tn), lambda l, n: (l, 0, n)),
        out_shape=jax.ShapeDtypeStruct((DEPTH, COND_ROWS, n6), F32),
        compiler_params=_params(2),
        name="mod_table",
    )(cond, mod_w, mod_b.reshape(DEPTH, 1, n6))
    return out.reshape(DEPTH, COND_ROWS, 6, 1, D_MODEL)


def _norm_mod_kernel(x_ref, g_ref, shift_ref, scale_ref, o_ref):
    x = x_ref[...]
    ms = jnp.mean(x * x, axis=-1, keepdims=True)
    y = x * lax.rsqrt(ms + NORM_EPS) * g_ref[...]
    o_ref[...] = (y * (1.0 + scale_ref[...]) + shift_ref[...]).astype(o_ref.dtype)


def norm_mod(x, gain, modt, layer, which_shift, which_scale, tok_off=0, tm=1024):
    n = x.shape[0]
    mod_spec = lambda which: pl.BlockSpec(
        (None, None, None, 1, D_MODEL), lambda m: (layer, _cond_row(tok_off + m * tm), which, 0, 0))
    return pl.pallas_call(
        _norm_mod_kernel,
        grid=(n // tm,),
        in_specs=[pl.BlockSpec((tm, D_MODEL), lambda m: (m, 0)),
                  pl.BlockSpec((None, 1, D_MODEL), lambda m: (layer, 0, 0)),
                  mod_spec(which_shift), mod_spec(which_scale)],
        out_specs=pl.BlockSpec((tm, D_MODEL), lambda m: (m, 0)),
        out_shape=jax.ShapeDtypeStruct((n, D_MODEL), BF16),
        compiler_params=_params(1),
        name="norm_mod",
    )(x, gain.reshape(-1, 1, D_MODEL), modt, modt)


def _final_norm_kernel(x_ref, g_ref, o_ref):
    x = x_ref[...]
    ms = jnp.mean(x * x, axis=-1, keepdims=True)
    o_ref[...] = x * lax.rsqrt(ms + NORM_EPS) * g_ref[...]


def final_rmsnorm(x, gain, row_off, rows, tm=1024):
    off = row_off // tm
    return pl.pallas_call(
        _final_norm_kernel,
        grid=(rows // tm,),
        in_specs=[pl.BlockSpec((tm, D_MODEL), lambda m: (off + m, 0)),
                  pl.BlockSpec((1, D_MODEL), lambda m: (0, 0))],
        out_specs=pl.BlockSpec((tm, D_MODEL), lambda m: (m, 0)),
        out_shape=jax.ShapeDtypeStruct((rows, D_MODEL), F32),
        compiler_params=_params(1),
        name="final_norm",
    )(x, gain.reshape(1, D_MODEL))


def _mm_kernel(*refs, nk, epilogue, scale, n_a, n_x, split):
    refs = list(refs)
    a_refs = [refs.pop(0) for _ in range(n_a)]
    w_ref = refs.pop(0)
    x_refs = [refs.pop(0) for _ in range(n_x)]
    gate_ref = refs.pop(0) if epilogue == "resid" else None
    o_ref = refs.pop(0)
    wb_ref = refs.pop(0)
    acc_ref = refs.pop(0) if nk > 1 else None
    m = pl.program_id(1)
    k = pl.program_id(2)

    def pick(parts):
        if len(parts) == 1:
            return parts[0][...]
        return jnp.where(m < split, parts[0][...], parts[1][...])

    @pl.when(m == 0)
    def _():
        wb_ref[k] = w_ref[...].astype(BF16)

    part = jnp.dot(pick(a_refs), wb_ref[k], preferred_element_type=F32)

    def finish(acc):
        if epilogue == "cast":
            if scale is not None:
                acc = acc * scale
            o_ref[...] = acc.astype(o_ref.dtype)
        elif epilogue == "relu2":
            r2 = jnp.maximum(acc, 0.0)
            o_ref[...] = (r2 * r2).astype(o_ref.dtype)
        else:
            o_ref[...] = pick(x_refs) + gate_ref[...] * acc

    if nk == 1:
        finish(part)
    else:
        @pl.when(k == 0)
        def _():
            acc_ref[...] = part

        @pl.when(jnp.logical_and(k > 0, k < nk - 1))
        def _():
            acc_ref[...] += part

        @pl.when(k == nk - 1)
        def _():
            finish(acc_ref[...] + part)


def matmul(a, w, *, layer, col_off, cols, tn, out_dtype, tm=1024, tk=2048, row_off=0, rows=None,
           epilogue="cast", scale=None, x=None, modt=None, mod_layer=None, which=None, single_buffer_w=False):
    a_parts = a if isinstance(a, tuple) else (a,)
    x_parts = () if x is None else (x if isinstance(x, tuple) else (x,))
    kdim = a_parts[0].shape[1]
    if rows is None:
        rows = sum(p.shape[0] for p in a_parts) - row_off
    nk, nn, nm = kdim // tk, cols // tn, rows // tm
    m_off, n_off = row_off // tm, col_off // tn
    split = N_CTX // tm
    cond_row = lambda m: _cond_row((m_off + m) * tm)

    def row_specs(parts, width, col_index):
        if len(parts) == 1:
            return [pl.BlockSpec((tm, width), lambda n, m, k: (m_off + m, col_index(n, m, k, True)))]
        return [pl.BlockSpec((tm, width), lambda n, m, k: (jnp.minimum(m, split - 1), col_index(n, m, k, m < split))),
                pl.BlockSpec((tm, width), lambda n, m, k: (jnp.maximum(m - split, 0), col_index(n, m, k, m >= split)))]

    in_specs = row_specs(a_parts, tk, lambda n, m, k, live: jnp.where(live, k, 0))
    args = list(a_parts)
    w_mode = dict(pipeline_mode=pl.Buffered(1)) if single_buffer_w else {}
    in_specs.append(
        pl.BlockSpec((None, tk, tn), lambda n, m, k: (layer, jnp.where(m == 0, k, nk - 1), n_off + n), **w_mode))
    args.append(w)
    if epilogue == "resid":
        in_specs += row_specs(x_parts, tn, lambda n, m, k, live: n)
        in_specs.append(pl.BlockSpec((None, None, None, 1, tn),
                                     lambda n, m, k: (mod_layer, cond_row(m), which, 0, n)))
        args += list(x_parts) + [modt]
    scratch = [pltpu.VMEM((nk, tk, tn), BF16)]
    if nk > 1:
        scratch.append(pltpu.VMEM((tm, tn), F32))
    return pl.pallas_call(
        functools.partial(_mm_kernel, nk=nk, epilogue=epilogue, scale=scale, n_a=len(a_parts),
                          n_x=len(x_parts), split=split),
        grid=(nn, nm, nk),
        in_specs=in_specs,
        out_specs=pl.BlockSpec((tm, tn), lambda n, m, k: (m, n)),
        out_shape=jax.ShapeDtypeStruct((rows, cols), out_dtype),
        scratch_shapes=scratch,
        compiler_params=_params(3),
        name="matmul_" + epilogue,
    )(*args)


def _ctx_attn_kernel(q_ref, k_ref, v_ref, o_ref):
    for h in range(NA_HEADS):
        sl = slice(h * NA_HEAD_DIM, (h + 1) * NA_HEAD_DIM)
        q = q_ref[:, sl]
        k = k_ref[:, sl].astype(BF16)
        v = v_ref[:, sl].astype(BF16)
        s = lax.dot_general(q, k, (((1,), (1,)), ((), ())), preferred_element_type=F32)
        p = jnp.exp(s - jnp.max(s, axis=-1, keepdims=True))
        l = jnp.sum(p, axis=-1, keepdims=True)
        o = jnp.dot(p.astype(BF16), v, preferred_element_type=F32) / l
        o_ref[:, sl] = o.astype(o_ref.dtype)


def ctx_attention(q, k, v):
    spec = pl.BlockSpec((SEQ, D_MODEL), lambda b: (b, 0))
    return pl.pallas_call(
        _ctx_attn_kernel,
        grid=(BATCH,),
        in_specs=[spec, spec, spec],
        out_specs=spec,
        out_shape=jax.ShapeDtypeStruct((N_CTX, D_MODEL), BF16),
        compiler_params=_params(1),
        name="ctx_attention",
    )(q, k, v)


GRID_ROWS = DEC_SEQ // GRID_W
Q_BLOCK_ROWS = 4
KEY_WIN_ROWS = 12
N_ROW_OFFS = 2 * WIN_ROWS - 1
TILE_BOTH, TILE_SECOND, TILE_FIRST = 0, N_ROW_OFFS - 1, 2 * N_ROW_OFFS - 1
TILE_NONE = 3 * N_ROW_OFFS - 1
N_BIAS_TILES = TILE_NONE + 1


def _row_start(r):
    return min(max(r - WIN_ROWS // 2, 0), GRID_ROWS - WIN_ROWS)


def _key_win_start(rb):
    return min(max(rb * Q_BLOCK_ROWS - WIN_ROWS // 2, 0), GRID_ROWS - KEY_WIN_ROWS)


def _na_bias_table(rpb):
    reach = WIN_COLS - 1
    qc = np.arange(GRID_W)[:, None]
    kc = np.arange(GRID_W)[None, :]
    ws = np.clip(qc - WIN_COLS // 2, 0, GRID_W - WIN_COLS)
    in_window = (kc >= ws) & (kc < ws + WIN_COLS)
    col_off = np.clip(kc - qc, -reach, reach) + reach
    pick = ((np.arange(2 * reach + 1)[:, None, None] == col_off[None]) & in_window[None]).astype(np.float32)
    t = jnp.einsum("hdk,kqc->hdqc", rpb, pick, precision=lax.Precision.HIGHEST)
    t = jnp.where(in_window, t, NEG_INF)
    masked = jnp.full_like(t, NEG_INF)
    return jnp.concatenate([
        jnp.concatenate([t[:, :-1], t[:, 1:]], axis=-1),
        jnp.concatenate([masked, t], axis=-1),
        jnp.concatenate([t, masked], axis=-1),
        jnp.concatenate([masked[:, :1], masked[:, :1]], axis=-1)], axis=1)


def _bias_tile_index(r, key_row):
    rs = _row_start(r)
    first_ok = rs <= key_row < rs + WIN_ROWS
    second_ok = rs <= key_row + 1 < rs + WIN_ROWS
    off = key_row - r + WIN_ROWS - 1
    if first_ok and second_ok:
        return TILE_BOTH + off
    if second_ok:
        return TILE_SECOND + off + 1
    if first_ok:
        return TILE_FIRST + off
    return TILE_NONE


def _na_kernel(q_ref, k_ref, v_ref, ck_ref, cv_ref, t_ref, o_ref):
    ck = ck_ref[...].astype(BF16)
    cv = cv_ref[...].astype(BF16)
    nt = (((1,), (1,)), ((), ()))
    blk = Q_BLOCK_ROWS * GRID_W
    for rb in range(GRID_ROWS // Q_BLOCK_ROWS):
        kw0 = _key_win_start(rb)
        keys = slice(kw0 * GRID_W, (kw0 + KEY_WIN_ROWS) * GRID_W)
        q = q_ref[rb * blk:(rb + 1) * blk, :]
        bias = jnp.concatenate([
            jnp.concatenate([t_ref[_bias_tile_index(rb * Q_BLOCK_ROWS + i, kw0 + 2 * p)]
                             for p in range(KEY_WIN_ROWS // 2)], axis=1)
            for i in range(Q_BLOCK_ROWS)], axis=0)
        s_loc = lax.dot_general(q, k_ref[keys, :], nt, preferred_element_type=F32) + bias
        s_ctx = lax.dot_general(q, ck, nt, preferred_element_type=F32)
        mx = jnp.maximum(jnp.max(s_loc, axis=-1, keepdims=True), jnp.max(s_ctx, axis=-1, keepdims=True))
        p_loc = jnp.exp(s_loc - mx)
        p_ctx = jnp.exp(s_ctx - mx)
        l = jnp.sum(p_loc, axis=-1, keepdims=True) + jnp.sum(p_ctx, axis=-1, keepdims=True)
        o = (jnp.dot(p_loc.astype(BF16), v_ref[keys, :], preferred_element_type=F32)
             + jnp.dot(p_ctx.astype(BF16), cv, preferred_element_type=F32)) / l
        o_ref[rb * blk:(rb + 1) * blk, :] = o.astype(o_ref.dtype)


def neighborhood_attention(q, k, v, cache_k, cache_v, cache_layer, bias_table):
    tok = pl.BlockSpec((DEC_SEQ, NA_HEAD_DIM), lambda h, b: (b, h))
    cache = pl.BlockSpec((None, PAST_LEN, NA_HEAD_DIM), lambda h, b: (b, cache_layer, h))
    return pl.pallas_call(
        _na_kernel,
        grid=(NA_HEADS, DEC_BATCH),
        in_specs=[tok, tok, tok, cache, cache,
                  pl.BlockSpec((None, N_BIAS_TILES, GRID_W, 2 * GRID_W), lambda h, b: (h, 0, 0, 0))],
        out_specs=tok,
        out_shape=jax.ShapeDtypeStruct((N_LAT, D_MODEL), BF16),
        compiler_params=_params(2),
        name="neighborhood_attention",
    )(q, k, v, cache_k, cache_v, bias_table)


N_SSD_INPUTS = 13


def _ssd_kernel(*refs, seq_len, has_h0, emit_state, seqs_per_step):
    n_in = N_SSD_INPUTS + (1 if has_h0 else 0)
    n_out = 1 + (1 if emit_state else 0)
    ins, outs, scratch = refs[:n_in], refs[n_in:n_in + n_out], refs[n_in + n_out:]
    nc = seq_len // SSM_CHUNK
    for s in range(seqs_per_step):
        tok = pl.ds(s * seq_len, seq_len)
        views = [ins[0].at[tok], ins[1].at[tok], ins[2].at[tok]] + list(ins[3:9])
        views += [ins[9].at[pl.ds(s * nc, nc)]] + list(ins[10:N_SSD_INPUTS])
        if has_h0:
            views.append(ins[N_SSD_INPUTS].at[s])
        views.append(outs[0].at[tok])
        if emit_state:
            views.append(outs[1].at[s])
        views += [r.at[s] for r in scratch]
        _ssd_sequence(*views, seq_len=seq_len, has_h0=has_h0, emit_state=emit_state)


def _ssd_sequence(*refs, seq_len, has_h0, emit_state):
    (x_ref, b_ref, c_ref, cwx_ref, cwb_ref, cwc_ref, cbx_ref, cbb_ref, cbc_ref,
     dt_ref, bias_ref, alog_ref, dsk_ref) = refs[:N_SSD_INPUTS]
    rest = list(refs[N_SSD_INPUTS:])
    h0_ref = rest.pop(0) if has_h0 else None
    y_ref = rest.pop(0)
    st_ref = rest.pop(0) if emit_state else None
    (xs_ref, bt_ref, cs_ref, act3_ref, rowp_ref, rowd_ref, cdb_ref, wr_ref, sloc_ref, hprev_ref, hcur_ref) = rest

    q = SSM_CHUNK
    nc = seq_len // q
    hp = lax.Precision.HIGHEST
    e_n = HEADS_PER_GROUP
    half = SSM_HEAD_DIM
    n_pairs = e_n // 2

    li = lax.broadcasted_iota(jnp.int32, (q, q), 0)
    si = lax.broadcasted_iota(jnp.int32, (q, q), 1)
    tril = (si <= li).astype(F32)
    triu = (si >= li).astype(F32)
    lower = si <= li
    diag = si == li
    first_half = si < half
    fwd_row = lax.broadcasted_iota(jnp.int32, (2 * e_n, q), 0) < e_n
    a_hm = -jnp.exp(alog_ref[...]) * LOG2E

    def chunk_rows(c):
        return pl.ds(c * q, q) if isinstance(c, int) else pl.ds(pl.multiple_of(c * q, q), q)

    def block_diag(pair):
        zero = jnp.zeros_like(pair)
        return jnp.concatenate([jnp.where(first_half, pair, zero), jnp.where(first_half, zero, pair)], axis=0)

    dt_all = _softplus(dt_ref[...] + bias_ref[...])
    da_all = (dt_all * a_hm).reshape(nc * 2 * e_n, q)
    fwd_rows = (lax.broadcasted_iota(jnp.int32, (nc * 2 * e_n, q), 0) % (2 * e_n)) < e_n
    ac_all = jnp.where(fwd_rows,
                       jnp.dot(da_all, triu, precision=hp, preferred_element_type=F32),
                       jnp.dot(da_all, tril, precision=hp, preferred_element_type=F32)).reshape(nc, 2 * e_n, q)
    rowp_ref[...] = ac_all - jnp.log2(dt_all)
    rowd_ref[...] = jnp.log2(dt_all[:, :e_n, :] + dt_all[:, e_n:, :])
    ends = jnp.broadcast_to(jnp.where(fwd_row[None, :, 0:1], ac_all[:, :, q - 1:q], ac_all[:, :, 0:1]), ac_all.shape)
    wr_ref[...] = jnp.exp2(ends - ac_all) * dt_all
    cdb_ref[...] = jnp.exp2(ends)
    hi = ac_all.astype(BF16).astype(F32)
    mid = (ac_all - hi).astype(BF16).astype(F32)
    lo = (ac_all - hi - mid).astype(BF16).astype(F32)
    for c in range(nc):
        act3_ref[c * q:(c + 1) * q, :] = jnp.concatenate([hi[c], mid[c], lo[c]], axis=0).T.astype(BF16)

    def prep(c, carry):
        rows = chunk_rows(c)
        r0 = c * q

        def conv_silu(u_ref, w_ref, bias_r):
            u = u_ref[rows, :].astype(F32)
            width = u.shape[1]
            before = u_ref[pl.ds(pl.multiple_of(jnp.maximum(r0 - 16, 0), 16), 16), :][15:16, :].astype(F32)
            after = u_ref[pl.ds(pl.multiple_of(jnp.minimum(r0 + q, seq_len - 16), 16), 16), :][0:1, :].astype(F32)
            before = jnp.where(c > 0, before, 0.0)
            after = jnp.where(c < nc - 1, after, 0.0)
            rid = lax.broadcasted_iota(jnp.int32, (q, width), 0)
            prev = jnp.where(rid == 0, before, pltpu.roll(u, 1, axis=0))
            nxt = jnp.where(rid == q - 1, after, pltpu.roll(u, q - 1, axis=0))
            w = w_ref[...] * 0.5
            h = prev * w[0:1] + u * w[1:2] + nxt * w[2:3] + bias_r[...] * 0.5
            return h + h * jnp.tanh(h)

        xc = conv_silu(x_ref, cwx_ref, cbx_ref).astype(BF16)
        xs_ref[rows, :] = xc
        cs_ref[rows, :] = conv_silu(c_ref, cwc_ref, cbc_ref).astype(BF16)
        bt = conv_silu(b_ref, cwb_ref, cbb_ref).T
        bt_ref[c] = bt
        wr = wr_ref[c]

        for j in range(n_pairs):
            lanes = slice(j * 2 * half, (j + 1) * 2 * half)
            xblk = block_diag(xc[:, lanes])
            e1, e2 = 2 * j, 2 * j + 1
            scaled = lambda r: (bt * wr[r:r + 1, :]).astype(BF16)
            lhs = jnp.concatenate([
                jnp.concatenate([scaled(e1), scaled(e2)], axis=1),
                jnp.concatenate([scaled(e_n + e1), scaled(e_n + e2)], axis=1)], axis=0)
            out = jnp.dot(lhs, xblk, preferred_element_type=F32)
            sloc_ref[c, 0, :, lanes] = out[:q]
            sloc_ref[c, 1, :, lanes] = out[q:]
        return carry

    unrolled = nc <= 2
    if unrolled:
        for c in range(nc):
            prep(c, 0)
    else:
        lax.fori_loop(0, nc, prep, 0, unroll=2)

    if has_h0:
        hcur_ref[0] = h0_ref[0].T
        hcur_ref[1] = h0_ref[1].T
    else:
        hcur_ref[...] = jnp.zeros(hcur_ref.shape, F32)

    def carry_states(direction):
        def body(i, carry):
            c = i if direction == 0 else nc - 1 - i
            h = hcur_ref[direction]
            hprev_ref[c, direction] = h.astype(BF16)
            cd = cdb_ref[c]
            r0 = direction * e_n
            decay = jnp.concatenate(
                [jnp.where(first_half[0:1, :], cd[r0 + 2 * j:r0 + 2 * j + 1, :], cd[r0 + 2 * j + 1:r0 + 2 * j + 2, :])
                 for j in range(n_pairs)], axis=1)
            hcur_ref[direction] = h * decay + sloc_ref[c, direction]
            return carry
        if unrolled:
            for i in range(nc):
                body(i, 0)
        else:
            lax.fori_loop(0, nc, body, 0)

    carry_states(0)
    carry_states(1)
    if emit_state:
        st_ref[0] = hcur_ref[0].T
        st_ref[1] = hcur_ref[1].T

    dsum = dsk_ref[0:1, :] + dsk_ref[1:2, :]
    sel_k = lax.broadcasted_iota(jnp.int32, (3 * 2 * e_n, 4 * q), 0) % (2 * e_n)
    sel_n = lax.broadcasted_iota(jnp.int32, (3 * 2 * e_n, 4 * q), 1) // q
    spread_sel = [(sel_k == 2 * j + (sel_n % 2) + e_n * (sel_n // 2)).astype(BF16) for j in range(n_pairs)]

    def emit(c, zero_in_f=False, zero_in_b=False):
        rows = chunk_rows(c)
        cc = cs_ref[rows, :]
        cb = jnp.dot(cc, bt_ref[c].astype(BF16), preferred_element_type=F32)
        off_f = None if zero_in_f else jnp.dot(cc, hprev_ref[c, 0], preferred_element_type=F32)
        off_b = None if zero_in_b else jnp.dot(cc, hprev_ref[c, 1], preferred_element_type=F32)
        act3 = act3_ref[rows, :]
        rp = rowp_ref[c]
        rd = rowd_ref[c]
        spreads = [jnp.dot(act3, spread_sel[j], preferred_element_type=F32) for j in range(n_pairs)]
        for j in range(n_pairs):
            lanes = slice(j * 2 * half, (j + 1) * 2 * half)
            spread = spreads[j]
            acf = [spread[:, 0:q], spread[:, q:2 * q]]
            acb = [spread[:, 2 * q:3 * q], spread[:, 3 * q:4 * q]]
            mix = []
            for i, e in enumerate((2 * j, 2 * j + 1)):
                arg = jnp.where(diag, rd[e:e + 1, :],
                                jnp.where(lower, acf[i] - rp[e:e + 1, :], acb[i] - rp[e_n + e:e_n + e + 1, :]))
                mix.append((cb * jnp.exp2(arg)).astype(BF16))
            xp = xs_ref[rows, lanes]
            y = jnp.dot(jnp.concatenate(mix, axis=1), block_diag(xp), preferred_element_type=F32)
            if off_f is not None:
                y = y + off_f[:, lanes] * jnp.exp2(jnp.where(first_half, acf[0], acf[1]))
            if off_b is not None:
                y = y + off_b[:, lanes] * jnp.exp2(jnp.where(first_half, acb[0], acb[1]))
            y_ref[rows, lanes] = (y + xp.astype(F32) * dsum[:, lanes]).astype(y_ref.dtype)

    if has_h0:
        def emit_body(c, carry):
            emit(c)
            return carry
        lax.fori_loop(0, nc, emit_body, 0, unroll=2)
    else:
        for c in range(nc):
            emit(c, zero_in_f=(c == 0), zero_in_b=(c == nc - 1))


def ssd_scan(proj, dt_hm, conv_w, conv_b, bias_hm, alog_hm, d_skip_wide, *, seq_len, n_seq, row_off,
             seqs_per_step=1, h0=None):
    has_h0 = h0 is not None
    emit_state = not has_h0
    nc = seq_len // SSM_CHUNK
    sps = seqs_per_step
    s_off = row_off // (sps * seq_len)
    gw = GROUP_WIDTH
    ns = SSM_STATE
    e2 = 2 * HEADS_PER_GROUP
    x_blk0 = SSM_D_INNER // gw
    b_blk0 = 2 * SSM_D_INNER // ns
    c_blk0 = b_blk0 + SSM_GROUPS
    cw_b0 = SSM_D_INNER // ns
    in_specs = [
        pl.BlockSpec((sps * seq_len, gw), lambda s, g: (s_off + s, x_blk0 + g)),
        pl.BlockSpec((sps * seq_len, ns), lambda s, g: (s_off + s, b_blk0 + g)),
        pl.BlockSpec((sps * seq_len, ns), lambda s, g: (s_off + s, c_blk0 + g)),
        pl.BlockSpec((None, 3, gw), lambda s, g: (0, 0, g)),
        pl.BlockSpec((None, 3, ns), lambda s, g: (0, 0, cw_b0 + g)),
        pl.BlockSpec((None, 3, ns), lambda s, g: (0, 0, cw_b0 + SSM_GROUPS + g)),
        pl.BlockSpec((None, 1, gw), lambda s, g: (0, 0, g)),
        pl.BlockSpec((None, 1, ns), lambda s, g: (0, 0, cw_b0 + g)),
        pl.BlockSpec((None, 1, ns), lambda s, g: (0, 0, cw_b0 + SSM_GROUPS + g)),
        pl.BlockSpec((None, sps * nc, e2, SSM_CHUNK), lambda s, g: (g, s_off + s, 0, 0)),
        pl.BlockSpec((None, e2, 1), lambda s, g: (g, 0, 0)),
        pl.BlockSpec((None, e2, 1), lambda s, g: (g, 0, 0)),
        pl.BlockSpec((2, gw), lambda s, g: (0, g)),
    ]
    args = [proj, proj, proj, conv_w, conv_w, conv_w, conv_b, conv_b, conv_b,
            dt_hm, bias_hm, alog_hm, d_skip_wide]
    assert len(args) == N_SSD_INPUTS
    state_spec = pl.BlockSpec((sps, 2, gw, ns), lambda s, g: (s, 0, g, 0))
    if has_h0:
        in_specs.append(state_spec)
        args.append(h0)
    y_spec = pl.BlockSpec((sps * seq_len, gw), lambda s, g: (s, g))
    y_shape = jax.ShapeDtypeStruct((n_seq * seq_len, SSM_D_INNER), BF16)
    if emit_state:
        out_specs = [y_spec, state_spec]
        out_shape = [y_shape, jax.ShapeDtypeStruct((n_seq, 2, SSM_D_INNER, ns), F32)]
    else:
        out_specs = y_spec
        out_shape = y_shape
    return pl.pallas_call(
        functools.partial(_ssd_kernel, seq_len=seq_len, has_h0=has_h0, emit_state=emit_state,
                          seqs_per_step=sps),
        grid=(n_seq // sps, SSM_GROUPS),
        in_specs=in_specs,
        out_specs=out_specs,
        out_shape=out_shape,
        scratch_shapes=[
            pltpu.VMEM((sps, seq_len, gw), BF16),
            pltpu.VMEM((sps, nc, ns, SSM_CHUNK), F32),
            pltpu.VMEM((sps, seq_len, ns), BF16),
            pltpu.VMEM((sps, seq_len, 3 * e2), BF16),
            pltpu.VMEM((sps, nc, e2, SSM_CHUNK), F32),
            pltpu.VMEM((sps, nc, HEADS_PER_GROUP, SSM_CHUNK), F32),
            pltpu.VMEM((sps, nc, e2, SSM_CHUNK), F32),
            pltpu.VMEM((sps, nc, e2, SSM_CHUNK), F32),
            pltpu.VMEM((sps, nc, 2, ns, gw), F32),
            pltpu.VMEM((sps, nc, 2, ns, gw), BF16),
            pltpu.VMEM((sps, 2, ns, gw), F32)],
        compiler_params=_params(2),
        name="ssd_scan",
    )(*args)


def _gate_norm_kernel(y_ref, z_ref, g_ref, o_ref):
    v = y_ref[...].astype(F32) * _silu(z_ref[...].astype(F32))
    ms = jnp.mean(v * v, axis=-1, keepdims=True)
    o_ref[...] = (v * lax.rsqrt(ms + NORM_EPS) * g_ref[...]).astype(o_ref.dtype)


def gate_norm(y, proj, gain, row_off, tm=512):
    rows = y.shape[0]
    off = row_off // tm
    return pl.pallas_call(
        _gate_norm_kernel,
        grid=(rows // tm,),
        in_specs=[pl.BlockSpec((tm, SSM_D_INNER), lambda m: (m, 0)),
                  pl.BlockSpec((tm, SSM_D_INNER), lambda m: (off + m, 0)),
                  pl.BlockSpec((1, SSM_D_INNER), lambda m: (0, 0))],
        out_specs=pl.BlockSpec((tm, SSM_D_INNER), lambda m: (m, 0)),
        out_shape=jax.ShapeDtypeStruct((rows, SSM_D_INNER), BF16),
        compiler_params=_params(1),
        name="gate_norm",
    )(y, proj, gain.reshape(1, SSM_D_INNER))


def _mlp(x, modt, layer, norm_mlp, mlp_w1, mlp_w2):
    h = norm_mod(x, norm_mlp, modt, layer, 3, 4)
    a = matmul(h, mlp_w1, layer=layer, col_off=0, cols=D_FF, tn=1024, tm=2048, out_dtype=BF16, epilogue="relu2")
    return matmul(a, mlp_w2, layer=layer, col_off=0, cols=D_MODEL, tn=512, tm=512, tk=D_FF, out_dtype=F32,
                  epilogue="resid", x=x, modt=modt, mod_layer=layer, which=5, single_buffer_w=True)


def kernel(x_prompt, x_sample, cache_attn_k, cache_attn_v, state_ssm, c, c_ctx, mod_w, mod_b, norm_mix, norm_mlp, mlp_w1, mlp_w2, na_wqkv, na_wo, na_rpb, ssm_w_in, ssm_conv_w, ssm_conv_b, ssm_dt_bias, ssm_a_log, ssm_d, ssm_norm, ssm_w_out, final_norm):
    x_ctx = x_prompt.reshape(N_CTX, D_MODEL)
    x_lat = x_sample.reshape(N_LAT, D_MODEL)
    cond = jnp.concatenate([c_ctx[None, :], c, jnp.zeros((COND_ROWS - 1 - DEC_BATCH, D_MODEL), F32)], axis=0)
    modt = mod_table(cond, mod_w, mod_b)

    h_ctx = norm_mod(x_ctx, norm_mix, modt, 0, 0, 1)
    h_lat = norm_mod(x_lat, norm_mix, modt, 0, 0, 1, tok_off=N_CTX)
    qkv = functools.partial(matmul, w=na_wqkv, layer=0, cols=D_MODEL, tn=1024)
    q_scale = NA_HEAD_DIM ** -0.5
    q_ctx = qkv(h_ctx, col_off=0, out_dtype=BF16, scale=q_scale, tm=2048)
    k_ctx = qkv(h_ctx, col_off=D_MODEL, out_dtype=F32)
    v_ctx = qkv(h_ctx, col_off=2 * D_MODEL, out_dtype=F32)
    q_lat = qkv(h_lat, col_off=0, out_dtype=BF16, scale=q_scale, tm=2048)
    k_lat = qkv(h_lat, col_off=D_MODEL, out_dtype=BF16, tm=2048)
    v_lat = qkv(h_lat, col_off=2 * D_MODEL, out_dtype=BF16, tm=2048)
    o_ctx = ctx_attention(q_ctx, k_ctx, v_ctx)
    n_attn = cache_attn_k.shape[1]
    o_lat = neighborhood_attention(
        q_lat, k_lat, v_lat,
        cache_attn_k.reshape(DEC_BATCH, n_attn * PAST_LEN, D_MODEL),
        cache_attn_v.reshape(DEC_BATCH, n_attn * PAST_LEN, D_MODEL),
        0, _na_bias_table(na_rpb[0]))
    x = matmul((o_ctx, o_lat), na_wo, layer=0, col_off=0, cols=D_MODEL, tn=1024, tm=512, out_dtype=F32,
               epilogue="resid", x=(x_ctx, x_lat), modt=modt, mod_layer=0, which=2)
    x = _mlp(x, modt, 0, norm_mlp, mlp_w1, mlp_w2)

    h = norm_mod(x, norm_mix, modt, 1, 0, 1)
    zxbc_cols = SSM_D_INNER + SSM_CONV_DIM
    proj = matmul(h, ssm_w_in, layer=0, col_off=0, cols=zxbc_cols, tn=1024, tm=2048, out_dtype=BF16)
    dt_raw = matmul(h, ssm_w_in, layer=0, col_off=zxbc_cols, cols=2 * SSM_HEADS, tn=128, out_dtype=F32)
    e_n = HEADS_PER_GROUP
    dt_hm = dt_raw.reshape(N_TOK // SSM_CHUNK, SSM_CHUNK, 2, SSM_GROUPS, e_n).transpose(3, 0, 2, 4, 1)
    dt_hm = dt_hm.reshape(SSM_GROUPS, N_TOK // SSM_CHUNK, 2 * e_n, SSM_CHUNK)
    by_group = lambda p: p.reshape(2, SSM_GROUPS, e_n).transpose(1, 0, 2).reshape(SSM_GROUPS, 2 * e_n, 1)
    n_ssm = state_ssm.shape[1]
    ssd = functools.partial(
        ssd_scan, proj, dt_hm, ssm_conv_w, ssm_conv_b.reshape(1, 1, SSM_CONV_DIM),
        by_group(ssm_dt_bias[0]), by_group(ssm_a_log[0]), jnp.repeat(ssm_d[0], SSM_HEAD_DIM, axis=1))
    y_ctx, new_state = ssd(seq_len=SEQ, n_seq=BATCH, row_off=0, seqs_per_step=4)
    y_lat = ssd(seq_len=DEC_SEQ, n_seq=DEC_BATCH, row_off=N_CTX,
                h0=state_ssm.reshape(DEC_BATCH, n_ssm * 2, SSM_D_INNER, SSM_STATE))
    yn = (gate_norm(y_ctx, proj, ssm_norm, 0), gate_norm(y_lat, proj, ssm_norm, N_CTX))
    x = matmul(yn, ssm_w_out, layer=0, col_off=0, cols=D_MODEL, tn=512, tm=512, tk=SSM_D_INNER, out_dtype=F32,
               epilogue="resid", x=x, modt=modt, mod_layer=1, which=2)
    x = _mlp(x, modt, 1, norm_mlp, mlp_w1, mlp_w2)

    y_prompt = final_rmsnorm(x, final_norm, 0, N_CTX).reshape(BATCH, SEQ, D_MODEL)
    y_sample = final_rmsnorm(x, final_norm, N_CTX, N_LAT).reshape(DEC_BATCH, DEC_SEQ, D_MODEL)
    kv_shape = (BATCH, 1, SEQ, NA_HEADS, NA_HEAD_DIM)
    return (y_prompt, y_sample, k_ctx.reshape(kv_shape), v_ctx.reshape(kv_shape),
            new_state.reshape(BATCH, 1, 2, SSM_HEADS, SSM_HEAD_DIM, SSM_STATE))
```

```python
import functools

import jax
import jax.numpy as jnp
import numpy as np
from jax import lax
from jax.experimental import pallas as pl
from jax.experimental.pallas import tpu as pltpu

F32 = jnp.float32
BF16 = jnp.bfloat16

D_MODEL = 2048
BATCH = 32
SEQ = 256
DEPTH = 2
DEC_BATCH = 4
DEC_SEQ = 1024
PAST_LEN = 256
GRID_W = 64
NA_HEADS = 16
NA_HEAD_DIM = 128
WIN_ROWS = 8
WIN_COLS = 16
SSM_D_INNER = 4096
SSM_HEAD_DIM = 64
SSM_HEADS = 64
SSM_GROUPS = 8
SSM_STATE = 128
SSM_CHUNK = 128
SSM_CONV_DIM = SSM_D_INNER + 2 * SSM_GROUPS * SSM_STATE
D_FF = 4 * D_MODEL
NORM_EPS = 1e-6
NEG_INF = -1e30
LOG2E = 1.4426950408889634

N_CTX = BATCH * SEQ
N_LAT = DEC_BATCH * DEC_SEQ
N_TOK = N_CTX + N_LAT
COND_ROWS = 8
HEADS_PER_GROUP = SSM_HEADS // SSM_GROUPS
GROUP_WIDTH = HEADS_PER_GROUP * SSM_HEAD_DIM

VMEM_LIMIT_BYTES = 56 * 1024 * 1024


def _params(n_axes):
    return pltpu.CompilerParams(dimension_semantics=("arbitrary",) * n_axes,
                                vmem_limit_bytes=VMEM_LIMIT_BYTES)


def _silu(v):
    h = 0.5 * v
    return h + h * jnp.tanh(h)


def _softplus(x):
    return jnp.maximum(x, 0.0) + jnp.log(1.0 + jnp.exp(-jnp.abs(x)))


def _cond_row(row_start):
    return jnp.where(row_start < N_CTX, 0, 1 + (row_start - N_CTX) // DEC_SEQ)


def _mod_kernel(cond_ref, w_ref, b_ref, o_ref):
    s = _silu(cond_ref[...]).astype(BF16)
    o_ref[...] = jnp.dot(s, w_ref[...].astype(BF16), preferred_element_type=F32) + b_ref[...]


def mod_table(cond, mod_w, mod_b):
    tn = 1024
    n6 = 6 * D_MODEL
    out = pl.pallas_call(
        _mod_kernel,
        grid=(DEPTH, n6 // tn),
        in_specs=[pl.BlockSpec((COND_ROWS, D_MODEL), lambda l, n: (0, 0)),
                  pl.BlockSpec((None, D_MODEL, tn), lambda l, n: (l, 0, n)),
                  pl.BlockSpec((None, 1, tn), lambda l, n: (l, 0, n))],
        out_specs=pl.BlockSpec((None, COND_ROWS, tn), lambda l, n: (l, 0, n)),
        out_shape=jax.ShapeDtypeStruct((DEPTH, COND_ROWS, n6), F32),
        compiler_params=_params(2),
        name="mod_table",
    )(cond, mod_w, mod_b.reshape(DEPTH, 1, n6))
    return out.reshape(DEPTH, COND_ROWS, 6, 1, D_MODEL)


def _norm_mod_kernel(x_ref, g_ref, shift_ref, scale_ref, o_ref):
    x = x_ref[...]
    ms = jnp.mean(x * x, axis=-1, keepdims=True)
    y = x * lax.rsqrt(ms + NORM_EPS) * g_ref[...]
    o_ref[...] = (y * (1.0 + scale_ref[...]) + shift_ref[...]).astype(o_ref.dtype)


def norm_mod(x, gain, modt, layer, which_shift, which_scale, tok_off=0, tm=1024):
    n = x.shape[0]
    mod_spec = lambda which: pl.BlockSpec(
        (None, None, None, 1, D_MODEL), lambda m: (layer, _cond_row(tok_off + m * tm), which, 0, 0))
    return pl.pallas_call(
        _norm_mod_kernel,
        grid=(n // tm,),
        in_specs=[pl.BlockSpec((tm, D_MODEL), lambda m: (m, 0)),
                  pl.BlockSpec((None, 1, D_MODEL), lambda m: (layer, 0, 0)),
                  mod_spec(which_shift), mod_spec(which_scale)],
        out_specs=pl.BlockSpec((tm, D_MODEL), lambda m: (m, 0)),
        out_shape=jax.ShapeDtypeStruct((n, D_MODEL), BF16),
        compiler_params=_params(1),
        name="norm_mod",
    )(x, gain.reshape(-1, 1, D_MODEL), modt, modt)


def _final_norm_kernel(x_ref, g_ref, o_ref):
    x = x_ref[...]
    ms = jnp.mean(x * x, axis=-1, keepdims=True)
    o_ref[...] = x * lax.rsqrt(ms + NORM_EPS) * g_ref[...]


def final_rmsnorm(x, gain, row_off, rows, tm=1024):
    off = row_off // tm
    return pl.pallas_call(
        _final_norm_kernel,
        grid=(rows // tm,),
        in_specs=[pl.BlockSpec((tm, D_MODEL), lambda m: (off + m, 0)),
                  pl.BlockSpec((1, D_MODEL), lambda m: (0, 0))],
        out_specs=pl.BlockSpec((tm, D_MODEL), lambda m: (m, 0)),
        out_shape=jax.ShapeDtypeStruct((rows, D_MODEL), F32),
        compiler_params=_params(1),
        name="final_norm",
    )(x, gain.reshape(1, D_MODEL))


def _mm_kernel(*refs, nk, epilogue, scale, n_a, n_x, split):
    refs = list(refs)
    a_refs = [refs.pop(0) for _ in range(n_a)]
    w_ref = refs.pop(0)
    x_refs = [refs.pop(0) for _ in range(n_x)]
    gate_ref = refs.pop(0) if epilogue == "resid" else None
    o_ref = refs.pop(0)
    wb_ref = refs.pop(0)
    acc_ref = refs.pop(0) if nk > 1 else None
    m = pl.program_id(1)
    k = pl.program_id(2)

    def pick(parts):
        if len(parts) == 1:
            return parts[0][...]
        return jnp.where(m < split, parts[0][...], parts[1][...])

    @pl.when(m == 0)
    def _():
        wb_ref[k] = w_ref[...].astype(BF16)

    part = jnp.dot(pick(a_refs), wb_ref[k], preferred_element_type=F32)

    def finish(acc):
        if epilogue == "cast":
            if scale is not None:
                acc = acc * scale
            o_ref[...] = acc.astype(o_ref.dtype)
        elif epilogue == "relu2":
            r2 = jnp.maximum(acc, 0.0)
            o_ref[...] = (r2 * r2).astype(o_ref.dtype)
        else:
            o_ref[...] = pick(x_refs) + gate_ref[...] * acc

    if nk == 1:
        finish(part)
    else:
        @pl.when(k == 0)
        def _():
            acc_ref[...] = part

        @pl.when(jnp.logical_and(k > 0, k < nk - 1))
        def _():
            acc_ref[...] += part

        @pl.when(k == nk - 1)
        def _():
            finish(acc_ref[...] + part)


def matmul(a, w, *, layer, col_off, cols, tn, out_dtype, tm=1024, tk=2048, row_off=0, rows=None,
           epilogue="cast", scale=None, x=None, modt=None, mod_layer=None, which=None, k_off=0, k_len=None):
    a_parts = a if isinstance(a, tuple) else (a,)
    x_parts = () if x is None else (x if isinstance(x, tuple) else (x,))
    kdim = a_parts[0].shape[1] if k_len is None else k_len
    if rows is None:
        rows = sum(p.shape[0] for p in a_parts) - row_off
    nk, nn, nm = kdim // tk, cols // tn, rows // tm
    m_off, n_off, kb_off = row_off // tm, col_off // tn, k_off // tk
    split = N_CTX // tm
    cond_row = lambda m: _cond_row((m_off + m) * tm)

    def row_specs(parts, width, col_index):
        if len(parts) == 1:
            return [pl.BlockSpec((tm, width), lambda n, m, k: (m_off + m, col_index(n, m, k, True)))]
        return [pl.BlockSpec((tm, width), lambda n, m, k: (jnp.minimum(m, split - 1), col_index(n, m, k, m < split))),
                pl.BlockSpec((tm, width), lambda n, m, k: (jnp.maximum(m - split, 0), col_index(n, m, k, m >= split)))]

    in_specs = row_specs(a_parts, tk, lambda n, m, k, live: kb_off + jnp.where(live, k, 0))
    args = list(a_parts)
    in_specs.append(pl.BlockSpec(
        (None, tk, tn), lambda n, m, k: (layer, kb_off + jnp.where(m == 0, k, nk - 1), n_off + n)))
    args.append(w)
    if epilogue == "resid":
        in_specs += row_specs(x_parts, tn, lambda n, m, k, live: n)
        in_specs.append(pl.BlockSpec((None, None, None, 1, tn),
                                     lambda n, m, k: (mod_layer, cond_row(m), which, 0, n)))
        args += list(x_parts) + [modt]
    scratch = [pltpu.VMEM((nk, tk, tn), BF16)]
    if nk > 1:
        scratch.append(pltpu.VMEM((tm, tn), F32))
    return pl.pallas_call(
        functools.partial(_mm_kernel, nk=nk, epilogue=epilogue, scale=scale, n_a=len(a_parts),
                          n_x=len(x_parts), split=split),
        grid=(nn, nm, nk),
        in_specs=in_specs,
        out_specs=pl.BlockSpec((tm, tn), lambda n, m, k: (m, n)),
        out_shape=jax.ShapeDtypeStruct((rows, cols), out_dtype),
        scratch_shapes=scratch,
        compiler_params=_params(3),
        name="matmul_" + epilogue,
    )(*args)


def _ctx_attn_kernel(q_ref, k_ref, v_ref, o_ref):
    for h in range(NA_HEADS):
        sl = slice(h * NA_HEAD_DIM, (h + 1) * NA_HEAD_DIM)
        q = q_ref[:, sl]
        k = k_ref[:, sl].astype(BF16)
        v = v_ref[:, sl].astype(BF16)
        s = lax.dot_general(q, k, (((1,), (1,)), ((), ())), preferred_element_type=F32)
        p = jnp.exp(s - jnp.max(s, axis=-1, keepdims=True))
        l = jnp.sum(p, axis=-1, keepdims=True)
        o = jnp.dot(p.astype(BF16), v, preferred_element_type=F32) / l
        o_ref[:, sl] = o.astype(o_ref.dtype)


def ctx_attention(q, k, v):
    spec = pl.BlockSpec((SEQ, D_MODEL), lambda b: (b, 0))
    return pl.pallas_call(
        _ctx_attn_kernel,
        grid=(BATCH,),
        in_specs=[spec, spec, spec],
        out_specs=spec,
        out_shape=jax.ShapeDtypeStruct((N_CTX, D_MODEL), BF16),
        compiler_params=_params(1),
        name="ctx_attention",
    )(q, k, v)


GRID_ROWS = DEC_SEQ // GRID_W
Q_BLOCK_ROWS = 4
KEY_WIN_ROWS = 12
N_ROW_OFFS = 2 * WIN_ROWS - 1
TILE_BOTH, TILE_SECOND, TILE_FIRST = 0, N_ROW_OFFS - 1, 2 * N_ROW_OFFS - 1
TILE_NONE = 3 * N_ROW_OFFS - 1
N_BIAS_TILES = TILE_NONE + 1


def _row_start(r):
    return min(max(r - WIN_ROWS // 2, 0), GRID_ROWS - WIN_ROWS)


def _key_win_start(rb):
    return min(max(rb * Q_BLOCK_ROWS - WIN_ROWS // 2, 0), GRID_ROWS - KEY_WIN_ROWS)


def _na_bias_table(rpb):
    reach = WIN_COLS - 1
    qc = np.arange(GRID_W)[:, None]
    kc = np.arange(GRID_W)[None, :]
    ws = np.clip(qc - WIN_COLS // 2, 0, GRID_W - WIN_COLS)
    in_window = (kc >= ws) & (kc < ws + WIN_COLS)
    col_off = np.clip(kc - qc, -reach, reach) + reach
    pick = ((np.arange(2 * reach + 1)[:, None, None] == col_off[None]) & in_window[None]).astype(np.float32)
    t = jnp.einsum("hdk,kqc->hdqc", rpb, pick, precision=lax.Precision.HIGHEST)
    t = jnp.where(in_window, t, NEG_INF)
    masked = jnp.full_like(t, NEG_INF)
    return jnp.concatenate([
        jnp.concatenate([t[:, :-1], t[:, 1:]], axis=-1),
        jnp.concatenate([masked, t], axis=-1),
        jnp.concatenate([t, masked], axis=-1),
        jnp.concatenate([masked[:, :1], masked[:, :1]], axis=-1)], axis=1)


def _bias_tile_index(r, key_row):
    rs = _row_start(r)
    first_ok = rs <= key_row < rs + WIN_ROWS
    second_ok = rs <= key_row + 1 < rs + WIN_ROWS
    off = key_row - r + WIN_ROWS - 1
    if first_ok and second_ok:
        return TILE_BOTH + off
    if second_ok:
        return TILE_SECOND + off + 1
    if first_ok:
        return TILE_FIRST + off
    return TILE_NONE


def _na_kernel(q_ref, k_ref, v_ref, ck_ref, cv_ref, t_ref, o_ref):
    ck = ck_ref[...].astype(BF16)
    cv = cv_ref[...].astype(BF16)
    nt = (((1,), (1,)), ((), ()))
    blk = Q_BLOCK_ROWS * GRID_W
    for rb in range(GRID_ROWS // Q_BLOCK_ROWS):
        kw0 = _key_win_start(rb)
        keys = slice(kw0 * GRID_W, (kw0 + KEY_WIN_ROWS) * GRID_W)
        q = q_ref[rb * blk:(rb + 1) * blk, :]
        bias = jnp.concatenate([
            jnp.concatenate([t_ref[_bias_tile_index(rb * Q_BLOCK_ROWS + i, kw0 + 2 * p)]
                             for p in range(KEY_WIN_ROWS // 2)], axis=1)
            for i in range(Q_BLOCK_ROWS)], axis=0)
        s_loc = lax.dot_general(q, k_ref[keys, :], nt, preferred_element_type=F32) + bias
        s_ctx = lax.dot_general(q, ck, nt, preferred_element_type=F32)
        mx = jnp.maximum(jnp.max(s_loc, axis=-1, keepdims=True), jnp.max(s_ctx, axis=-1, keepdims=True))
        p_loc = jnp.exp(s_loc - mx)
        p_ctx = jnp.exp(s_ctx - mx)
        l = jnp.sum(p_loc, axis=-1, keepdims=True) + jnp.sum(p_ctx, axis=-1, keepdims=True)
        o = (jnp.dot(p_loc.astype(BF16), v_ref[keys, :], preferred_element_type=F32)
             + jnp.dot(p_ctx.astype(BF16), cv, preferred_element_type=F32)) / l
        o_ref[rb * blk:(rb + 1) * blk, :] = o.astype(o_ref.dtype)


def neighborhood_attention(q, k, v, cache_k, cache_v, cache_layer, bias_table):
    tok = pl.BlockSpec((DEC_SEQ, NA_HEAD_DIM), lambda h, b: (b, h))
    cache = pl.BlockSpec((None, PAST_LEN, NA_HEAD_DIM), lambda h, b: (b, cache_layer, h))
    return pl.pallas_call(
        _na_kernel,
        grid=(NA_HEADS, DEC_BATCH),
        in_specs=[tok, tok, tok, cache, cache,
                  pl.BlockSpec((None, N_BIAS_TILES, GRID_W, 2 * GRID_W), lambda h, b: (h, 0, 0, 0))],
        out_specs=tok,
        out_shape=jax.ShapeDtypeStruct((N_LAT, D_MODEL), BF16),
        compiler_params=_params(2),
        name="neighborhood_attention",
    )(q, k, v, cache_k, cache_v, bias_table)


N_SSD_INPUTS = 13


def _ssd_kernel(*refs, seq_len, has_h0, emit_state, seqs_per_step):
    n_in = N_SSD_INPUTS + (1 if has_h0 else 0)
    n_out = 1 + (1 if emit_state else 0)
    ins, outs, scratch = refs[:n_in], refs[n_in:n_in + n_out], refs[n_in + n_out:]
    nc = seq_len // SSM_CHUNK
    for s in range(seqs_per_step):
        tok = pl.ds(s * seq_len, seq_len)
        views = [ins[0].at[tok], ins[1].at[tok], ins[2].at[tok]] + list(ins[3:9])
        views += [ins[9].at[pl.ds(s * nc, nc)]] + list(ins[10:N_SSD_INPUTS])
        if has_h0:
            views.append(ins[N_SSD_INPUTS].at[s])
        views.append(outs[0].at[tok])
        if emit_state:
            views.append(outs[1].at[s])
        views += [r.at[s] for r in scratch]
        _ssd_sequence(*views, seq_len=seq_len, has_h0=has_h0, emit_state=emit_state)


def _ssd_sequence(*refs, seq_len, has_h0, emit_state):
    (x_ref, b_ref, c_ref, cwx_ref, cwb_ref, cwc_ref, cbx_ref, cbb_ref, cbc_ref,
     dt_ref, bias_ref, alog_ref, dsk_ref) = refs[:N_SSD_INPUTS]
    rest = list(refs[N_SSD_INPUTS:])
    h0_ref = rest.pop(0) if has_h0 else None
    y_ref = rest.pop(0)
    st_ref = rest.pop(0) if emit_state else None
    (xs_ref, bt_ref, cs_ref, act3_ref, rowp_ref, rowd_ref, cdb_ref, wr_ref, sloc_ref, hprev_ref, hcur_ref) = rest

    q = SSM_CHUNK
    nc = seq_len // q
    hp = lax.Precision.HIGHEST
    e_n = HEADS_PER_GROUP
    half = SSM_HEAD_DIM
    n_pairs = e_n // 2

    li = lax.broadcasted_iota(jnp.int32, (q, q), 0)
    si = lax.broadcasted_iota(jnp.int32, (q, q), 1)
    tril = (si <= li).astype(F32)
    triu = (si >= li).astype(F32)
    lower = si <= li
    diag = si == li
    first_half = si < half
    fwd_row = lax.broadcasted_iota(jnp.int32, (2 * e_n, q), 0) < e_n
    a_hm = -jnp.exp(alog_ref[...]) * LOG2E

    def chunk_rows(c):
        return pl.ds(c * q, q) if isinstance(c, int) else pl.ds(pl.multiple_of(c * q, q), q)

    def block_diag(pair):
        zero = jnp.zeros_like(pair)
        return jnp.concatenate([jnp.where(first_half, pair, zero), jnp.where(first_half, zero, pair)], axis=0)

    dt_all = _softplus(dt_ref[...] + bias_ref[...])
    da_all = (dt_all * a_hm).reshape(nc * 2 * e_n, q)
    fwd_rows = (lax.broadcasted_iota(jnp.int32, (nc * 2 * e_n, q), 0) % (2 * e_n)) < e_n
    ac_all = jnp.where(fwd_rows,
                       jnp.dot(da_all, triu, precision=hp, preferred_element_type=F32),
                       jnp.dot(da_all, tril, precision=hp, preferred_element_type=F32)).reshape(nc, 2 * e_n, q)
    rowp_ref[...] = ac_all - jnp.log2(dt_all)
    rowd_ref[...] = jnp.log2(dt_all[:, :e_n, :] + dt_all[:, e_n:, :])
    ends = jnp.broadcast_to(jnp.where(fwd_row[None, :, 0:1], ac_all[:, :, q - 1:q], ac_all[:, :, 0:1]), ac_all.shape)
    wr_ref[...] = jnp.exp2(ends - ac_all) * dt_all
    cdb_ref[...] = jnp.exp2(ends)
    hi = ac_all.astype(BF16).astype(F32)
    mid = (ac_all - hi).astype(BF16).astype(F32)
    lo = (ac_all - hi - mid).astype(BF16).astype(F32)
    for c in range(nc):
        act3_ref[c * q:(c + 1) * q, :] = jnp.concatenate([hi[c], mid[c], lo[c]], axis=0).T.astype(BF16)

    def prep(c, carry):
        rows = chunk_rows(c)
        r0 = c * q

        def conv_silu(u_ref, w_ref, bias_r):
            u = u_ref[rows, :].astype(F32)
            width = u.shape[1]
            before = u_ref[pl.ds(pl.multiple_of(jnp.maximum(r0 - 16, 0), 16), 16), :][15:16, :].astype(F32)
            after = u_ref[pl.ds(pl.multiple_of(jnp.minimum(r0 + q, seq_len - 16), 16), 16), :][0:1, :].astype(F32)
            before = jnp.where(c > 0, before, 0.0)
            after = jnp.where(c < nc - 1, after, 0.0)
            rid = lax.broadcasted_iota(jnp.int32, (q, width), 0)
            prev = jnp.where(rid == 0, before, pltpu.roll(u, 1, axis=0))
            nxt = jnp.where(rid == q - 1, after, pltpu.roll(u, q - 1, axis=0))
            w = w_ref[...] * 0.5
            h = prev * w[0:1] + u * w[1:2] + nxt * w[2:3] + bias_r[...] * 0.5
            return h + h * jnp.tanh(h)

        xc = conv_silu(x_ref, cwx_ref, cbx_ref).astype(BF16)
        xs_ref[rows, :] = xc
        cs_ref[rows, :] = conv_silu(c_ref, cwc_ref, cbc_ref).astype(BF16)
        bt = conv_silu(b_ref, cwb_ref, cbb_ref).T
        bt_ref[c] = bt
        wr = wr_ref[c]

        for j in range(n_pairs):
            lanes = slice(j * 2 * half, (j + 1) * 2 * half)
            xblk = block_diag(xc[:, lanes])
            e1, e2 = 2 * j, 2 * j + 1
            scaled = lambda r: (bt * wr[r:r + 1, :]).astype(BF16)
            lhs = jnp.concatenate([
                jnp.concatenate([scaled(e1), scaled(e2)], axis=1),
                jnp.concatenate([scaled(e_n + e1), scaled(e_n + e2)], axis=1)], axis=0)
            out = jnp.dot(lhs, xblk, preferred_element_type=F32)
            sloc_ref[c, 0, :, lanes] = out[:q]
            sloc_ref[c, 1, :, lanes] = out[q:]
        return carry

    unrolled = nc <= 2
    if unrolled:
        for c in range(nc):
            prep(c, 0)
    else:
        lax.fori_loop(0, nc, prep, 0, unroll=2)

    if has_h0:
        hcur_ref[0] = h0_ref[0].T
        hcur_ref[1] = h0_ref[1].T
    else:
        hcur_ref[...] = jnp.zeros(hcur_ref.shape, F32)

    def carry_states(direction):
        def body(i, carry):
            c = i if direction == 0 else nc - 1 - i
            h = hcur_ref[direction]
            hprev_ref[c, direction] = h.astype(BF16)
            cd = cdb_ref[c]
            r0 = direction * e_n
            decay = jnp.concatenate(
                [jnp.where(first_half[0:1, :], cd[r0 + 2 * j:r0 + 2 * j + 1, :], cd[r0 + 2 * j + 1:r0 + 2 * j + 2, :])
                 for j in range(n_pairs)], axis=1)
            hcur_ref[direction] = h * decay + sloc_ref[c, direction]
            return carry
        if unrolled:
            for i in range(nc):
                body(i, 0)
        else:
            lax.fori_loop(0, nc, body, 0)

    carry_states(0)
    carry_states(1)
    if emit_state:
        st_ref[0] = hcur_ref[0].T
        st_ref[1] = hcur_ref[1].T

    dsum = dsk_ref[0:1, :] + dsk_ref[1:2, :]
    sel_k = lax.broadcasted_iota(jnp.int32, (3 * 2 * e_n, 4 * q), 0) % (2 * e_n)
    sel_n = lax.broadcasted_iota(jnp.int32, (3 * 2 * e_n, 4 * q), 1) // q
    spread_sel = [(sel_k == 2 * j + (sel_n % 2) + e_n * (sel_n // 2)).astype(BF16) for j in range(n_pairs)]

    def emit(c, zero_in_f=False, zero_in_b=False):
        rows = chunk_rows(c)
        cc = cs_ref[rows, :]
        cb = jnp.dot(cc, bt_ref[c].astype(BF16), preferred_element_type=F32)
        off_f = None if zero_in_f else jnp.dot(cc, hprev_ref[c, 0], preferred_element_type=F32)
        off_b = None if zero_in_b else jnp.dot(cc, hprev_ref[c, 1], preferred_element_type=F32)
        act3 = act3_ref[rows, :]
        rp = rowp_ref[c]
        rd = rowd_ref[c]
        spreads = [jnp.dot(act3, spread_sel[j], preferred_element_type=F32) for j in range(n_pairs)]
        for j in range(n_pairs):
            lanes = slice(j * 2 * half, (j + 1) * 2 * half)
            spread = spreads[j]
            acf = [spread[:, 0:q], spread[:, q:2 * q]]
            acb = [spread[:, 2 * q:3 * q], spread[:, 3 * q:4 * q]]
            mix = []
            for i, e in enumerate((2 * j, 2 * j + 1)):
                arg = jnp.where(diag, rd[e:e + 1, :],
                                jnp.where(lower, acf[i] - rp[e:e + 1, :], acb[i] - rp[e_n + e:e_n + e + 1, :]))
                mix.append((cb * jnp.exp2(arg)).astype(BF16))
            xp = xs_ref[rows, lanes]
            y = jnp.dot(jnp.concatenate(mix, axis=1), block_diag(xp), preferred_element_type=F32)
            if off_f is not None:
                y = y + off_f[:, lanes] * jnp.exp2(jnp.where(first_half, acf[0], acf[1]))
            if off_b is not None:
                y = y + off_b[:, lanes] * jnp.exp2(jnp.where(first_half, acb[0], acb[1]))
            y_ref[rows, lanes] = (y + xp.astype(F32) * dsum[:, lanes]).astype(y_ref.dtype)

    if has_h0:
        def emit_body(c, carry):
            emit(c)
            return carry
        lax.fori_loop(0, nc, emit_body, 0, unroll=2)
    else:
        for c in range(nc):
            emit(c, zero_in_f=(c == 0), zero_in_b=(c == nc - 1))


def ssd_scan(proj, dt_hm, conv_w, conv_b, bias_hm, alog_hm, d_skip_wide, *, seq_len, n_seq, row_off,
             seqs_per_step=1, h0=None):
    has_h0 = h0 is not None
    emit_state = not has_h0
    nc = seq_len // SSM_CHUNK
    sps = seqs_per_step
    s_off = row_off // (sps * seq_len)
    gw = GROUP_WIDTH
    ns = SSM_STATE
    e2 = 2 * HEADS_PER_GROUP
    x_blk0 = SSM_D_INNER // gw
    b_blk0 = 2 * SSM_D_INNER // ns
    c_blk0 = b_blk0 + SSM_GROUPS
    cw_b0 = SSM_D_INNER // ns
    in_specs = [
        pl.BlockSpec((sps * seq_len, gw), lambda s, g: (s_off + s, x_blk0 + g)),
        pl.BlockSpec((sps * seq_len, ns), lambda s, g: (s_off + s, b_blk0 + g)),
        pl.BlockSpec((sps * seq_len, ns), lambda s, g: (s_off + s, c_blk0 + g)),
        pl.BlockSpec((None, 3, gw), lambda s, g: (0, 0, g)),
        pl.BlockSpec((None, 3, ns), lambda s, g: (0, 0, cw_b0 + g)),
        pl.BlockSpec((None, 3, ns), lambda s, g: (0, 0, cw_b0 + SSM_GROUPS + g)),
        pl.BlockSpec((None, 1, gw), lambda s, g: (0, 0, g)),
        pl.BlockSpec((None, 1, ns), lambda s, g: (0, 0, cw_b0 + g)),
        pl.BlockSpec((None, 1, ns), lambda s, g: (0, 0, cw_b0 + SSM_GROUPS + g)),
        pl.BlockSpec((None, sps * nc, e2, SSM_CHUNK), lambda s, g: (g, s_off + s, 0, 0)),
        pl.BlockSpec((None, e2, 1), lambda s, g: (g, 0, 0)),
        pl.BlockSpec((None, e2, 1), lambda s, g: (g, 0, 0)),
        pl.BlockSpec((2, gw), lambda s, g: (0, g)),
    ]
    args = [proj, proj, proj, conv_w, conv_w, conv_w, conv_b, conv_b, conv_b,
            dt_hm, bias_hm, alog_hm, d_skip_wide]
    assert len(args) == N_SSD_INPUTS
    state_spec = pl.BlockSpec((sps, 2, gw, ns), lambda s, g: (s, 0, g, 0))
    if has_h0:
        in_specs.append(state_spec)
        args.append(h0)
    y_spec = pl.BlockSpec((sps * seq_len, gw), lambda s, g: (s, g))
    y_shape = jax.ShapeDtypeStruct((n_seq * seq_len, SSM_D_INNER), BF16)
    if emit_state:
        out_specs = [y_spec, state_spec]
        out_shape = [y_shape, jax.ShapeDtypeStruct((n_seq, 2, SSM_D_INNER, ns), F32)]
    else:
        out_specs = y_spec
        out_shape = y_shape
    return pl.pallas_call(
        functools.partial(_ssd_kernel, seq_len=seq_len, has_h0=has_h0, emit_state=emit_state,
                          seqs_per_step=sps),
        grid=(n_seq // sps, SSM_GROUPS),
        in_specs=in_specs,
        out_specs=out_specs,
        out_shape=out_shape,
        scratch_shapes=[
            pltpu.VMEM((sps, seq_len, gw), BF16),
            pltpu.VMEM((sps, nc, ns, SSM_CHUNK), F32),
            pltpu.VMEM((sps, seq_len, ns), BF16),
            pltpu.VMEM((sps, seq_len, 3 * e2), BF16),
            pltpu.VMEM((sps, nc, e2, SSM_CHUNK), F32),
            pltpu.VMEM((sps, nc, HEADS_PER_GROUP, SSM_CHUNK), F32),
            pltpu.VMEM((sps, nc, e2, SSM_CHUNK), F32),
            pltpu.VMEM((sps, nc, e2, SSM_CHUNK), F32),
            pltpu.VMEM((sps, nc, 2, ns, gw), F32),
            pltpu.VMEM((sps, nc, 2, ns, gw), BF16),
            pltpu.VMEM((sps, 2, ns, gw), F32)],
        compiler_params=_params(2),
        name="ssd_scan",
    )(*args)


def _gate_norm_kernel(y_ctx_ref, y_lat_ref, z_ref, g_ref, o_ref, *, split):
    y = jnp.where(pl.program_id(0) < split, y_ctx_ref[...], y_lat_ref[...])
    v = y.astype(F32) * _silu(z_ref[...].astype(F32))
    ms = jnp.mean(v * v, axis=-1, keepdims=True)
    o_ref[...] = (v * lax.rsqrt(ms + NORM_EPS) * g_ref[...]).astype(o_ref.dtype)


def gate_norm(y_ctx, y_lat, proj, gain, tm=512):
    split = N_CTX // tm
    wide = lambda index: pl.BlockSpec((tm, SSM_D_INNER), index)
    return pl.pallas_call(
        functools.partial(_gate_norm_kernel, split=split),
        grid=(N_TOK // tm,),
        in_specs=[wide(lambda m: (jnp.minimum(m, split - 1), 0)),
                  wide(lambda m: (jnp.maximum(m - split, 0), 0)),
                  wide(lambda m: (m, 0)),
                  pl.BlockSpec((1, SSM_D_INNER), lambda m: (0, 0))],
        out_specs=wide(lambda m: (m, 0)),
        out_shape=jax.ShapeDtypeStruct((N_TOK, SSM_D_INNER), BF16),
        compiler_params=_params(1),
        name="gate_norm",
    )(y_ctx, y_lat, proj, gain.reshape(1, SSM_D_INNER))


def _mlp(x, modt, layer, norm_mlp, mlp_w1, mlp_w2):
    h = norm_mod(x, norm_mlp, modt, layer, 3, 4)
    a = matmul(h, mlp_w1, layer=layer, col_off=0, cols=D_FF, tn=1024, tm=2048, out_dtype=BF16, epilogue="relu2")
    half = D_FF // 2
    for k_off in (0, half):
        x = matmul(a, mlp_w2, layer=layer, col_off=0, cols=D_MODEL, tn=512, tk=half, k_off=k_off, k_len=half,
                   out_dtype=F32, epilogue="resid", x=x, modt=modt, mod_layer=layer, which=5)
    return x


def kernel(x_prompt, x_sample, cache_attn_k, cache_attn_v, state_ssm, c, c_ctx, mod_w, mod_b, norm_mix, norm_mlp, mlp_w1, mlp_w2, na_wqkv, na_wo, na_rpb, ssm_w_in, ssm_conv_w, ssm_conv_b, ssm_dt_bias, ssm_a_log, ssm_d, ssm_norm, ssm_w_out, final_norm):
    x_ctx = x_prompt.reshape(N_CTX, D_MODEL)
    x_lat = x_sample.reshape(N_LAT, D_MODEL)
    cond = jnp.concatenate([c_ctx[None, :], c, jnp.zeros((COND_ROWS - 1 - DEC_BATCH, D_MODEL), F32)], axis=0)
    modt = mod_table(cond, mod_w, mod_b)

    h_ctx = norm_mod(x_ctx, norm_mix, modt, 0, 0, 1)
    h_lat = norm_mod(x_lat, norm_mix, modt, 0, 0, 1, tok_off=N_CTX)
    qkv = functools.partial(matmul, w=na_wqkv, layer=0, cols=D_MODEL, tn=1024)
    q_scale = NA_HEAD_DIM ** -0.5
    q_ctx = qkv(h_ctx, col_off=0, out_dtype=BF16, scale=q_scale, tm=2048)
    k_ctx = qkv(h_ctx, col_off=D_MODEL, out_dtype=F32)
    v_ctx = qkv(h_ctx, col_off=2 * D_MODEL, out_dtype=F32)
    q_lat = qkv(h_lat, col_off=0, out_dtype=BF16, scale=q_scale, tm=2048)
    k_lat = qkv(h_lat, col_off=D_MODEL, out_dtype=BF16, tm=2048)
    v_lat = qkv(h_lat, col_off=2 * D_MODEL, out_dtype=BF16, tm=2048)
    o_ctx = ctx_attention(q_ctx, k_ctx, v_ctx)
    n_attn = cache_attn_k.shape[1]
    o_lat = neighborhood_attention(
        q_lat, k_lat, v_lat,
        cache_attn_k.reshape(DEC_BATCH, n_attn * PAST_LEN, D_MODEL),
        cache_attn_v.reshape(DEC_BATCH, n_attn * PAST_LEN, D_MODEL),
        0, _na_bias_table(na_rpb[0]))
    x = matmul((o_ctx, o_lat), na_wo, layer=0, col_off=0, cols=D_MODEL, tn=1024, tm=512, out_dtype=F32,
               epilogue="resid", x=(x_ctx, x_lat), modt=modt, mod_layer=0, which=2)
    x = _mlp(x, modt, 0, norm_mlp, mlp_w1, mlp_w2)

    h = norm_mod(x, norm_mix, modt, 1, 0, 1)
    zxbc_cols = SSM_D_INNER + SSM_CONV_DIM
    proj = matmul(h, ssm_w_in, layer=0, col_off=0, cols=zxbc_cols, tn=1024, tm=2048, out_dtype=BF16)
    dt_raw = matmul(h, ssm_w_in, layer=0, col_off=zxbc_cols, cols=2 * SSM_HEADS, tn=128, out_dtype=F32)
    e_n = HEADS_PER_GROUP
    dt_hm = dt_raw.reshape(N_TOK // SSM_CHUNK, SSM_CHUNK, 2, SSM_GROUPS, e_n).transpose(3, 0, 2, 4, 1)
    dt_hm = dt_hm.reshape(SSM_GROUPS, N_TOK // SSM_CHUNK, 2 * e_n, SSM_CHUNK)
    by_group = lambda p: p.reshape(2, SSM_GROUPS, e_n).transpose(1, 0, 2).reshape(SSM_GROUPS, 2 * e_n, 1)
    n_ssm = state_ssm.shape[1]
    ssd = functools.partial(
        ssd_scan, proj, dt_hm, ssm_conv_w, ssm_conv_b.reshape(1, 1, SSM_CONV_DIM),
        by_group(ssm_dt_bias[0]), by_group(ssm_a_log[0]), jnp.repeat(ssm_d[0], SSM_HEAD_DIM, axis=1))
    y_ctx, new_state = ssd(seq_len=SEQ, n_seq=BATCH, row_off=0, seqs_per_step=4)
    y_lat = ssd(seq_len=DEC_SEQ, n_seq=DEC_BATCH, row_off=N_CTX,
                h0=state_ssm.reshape(DEC_BATCH, n_ssm * 2, SSM_D_INNER, SSM_STATE))
    yn = gate_norm(y_ctx, y_lat, proj, ssm_norm)
    x = matmul(yn, ssm_w_out, layer=0, col_off=0, cols=D_MODEL, tn=512, tk=SSM_D_INNER, out_dtype=F32,
               epilogue="resid", x=x, modt=modt, mod_layer=1, which=2)
    x = _mlp(x, modt, 1, norm_mlp, mlp_w1, mlp_w2)

    y_prompt = final_rmsnorm(x, final_norm, 0, N_CTX).reshape(BATCH, SEQ, D_MODEL)
    y_sample = final_rmsnorm(x, final_norm, N_CTX, N_LAT).reshape(DEC_BATCH, DEC_SEQ, D_MODEL)
    kv_shape = (BATCH, 1, SEQ, NA_HEADS, NA_HEAD_DIM)
    return (y_prompt, y_sample, k_ctx.reshape(kv_shape), v_ctx.reshape(kv_shape),
            new_state.reshape(BATCH, 1, 2, SSM_HEADS, SSM_HEAD_DIM, SSM_STATE))
```

```python
import functools

import jax
import jax.numpy as jnp
import numpy as np
from jax import lax
from jax.experimental import pallas as pl
from jax.experimental.pallas import tpu as pltpu

F32 = jnp.float32
BF16 = jnp.bfloat16

D_MODEL = 2048
BATCH = 32
SEQ = 256
DEPTH = 2
DEC_BATCH = 4
DEC_SEQ = 1024
PAST_LEN = 256
GRID_W = 64
NA_HEADS = 16
NA_HEAD_DIM = 128
WIN_ROWS = 8
WIN_COLS = 16
SSM_D_INNER = 4096
SSM_HEAD_DIM = 64
SSM_HEADS = 64
SSM_GROUPS = 8
SSM_STATE = 128
SSM_CHUNK = 128
SSM_CONV_DIM = SSM_D_INNER + 2 * SSM_GROUPS * SSM_STATE
D_FF = 4 * D_MODEL
NORM_EPS = 1e-6
NEG_INF = -1e30
LOG2E = 1.4426950408889634

N_CTX = BATCH * SEQ
N_LAT = DEC_BATCH * DEC_SEQ
N_TOK = N_CTX + N_LAT
COND_ROWS = 8
HEADS_PER_GROUP = SSM_HEADS // SSM_GROUPS
GROUP_WIDTH = HEADS_PER_GROUP * SSM_HEAD_DIM

VMEM_LIMIT_BYTES = 56 * 1024 * 1024


def _params(n_axes):
    return pltpu.CompilerParams(dimension_semantics=("arbitrary",) * n_axes,
                                vmem_limit_bytes=VMEM_LIMIT_BYTES)


def _silu(v):
    h = 0.5 * v
    return h + h * jnp.tanh(h)


def _softplus(x):
    return jnp.maximum(x, 0.0) + jnp.log(1.0 + jnp.exp(-jnp.abs(x)))


def _cond_row(row_start):
    return jnp.where(row_start < N_CTX, 0, 1 + (row_start - N_CTX) // DEC_SEQ)


def _mod_kernel(cond_ref, w_ref, b_ref, o_ref):
    s = _silu(cond_ref[...]).astype(BF16)
    o_ref[...] = jnp.dot(s, w_ref[...].astype(BF16), preferred_element_type=F32) + b_ref[...]


def mod_table(cond, mod_w, mod_b):
    tn = 2048
    n6 = 6 * D_MODEL
    out = pl.pallas_call(
        _mod_kernel,
        grid=(DEPTH, n6 // tn),
        in_specs=[pl.BlockSpec((COND_ROWS, D_MODEL), lambda l, n: (0, 0)),
                  pl.BlockSpec((None, D_MODEL, tn), lambda l, n: (l, 0, n)),
                  pl.BlockSpec((None, 1, tn), lambda l, n: (l, 0, n))],
        out_specs=pl.BlockSpec((None, COND_ROWS, tn), lambda l, n: (l, 0, n)),
        out_shape=jax.ShapeDtypeStruct((DEPTH, COND_ROWS, n6), F32),
        compiler_params=_params(2),
        name="mod_table",
    )(cond, mod_w, mod_b.reshape(DEPTH, 1, n6))
    return out.reshape(DEPTH, COND_ROWS, 6, 1, D_MODEL)


def _norm_mod_kernel(x_ref, g_ref, shift_ref, scale_ref, o_ref):
    x = x_ref[...]
    ms = jnp.mean(x * x, axis=-1, keepdims=True)
    y = x * lax.rsqrt(ms + NORM_EPS) * g_ref[...]
    o_ref[...] = (y * (1.0 + scale_ref[...]) + shift_ref[...]).astype(o_ref.dtype)


def norm_mod(x, gain, modt, layer, which_shift, which_scale, tok_off=0, tm=1024):
    n = x.shape[0]
    mod_spec = lambda which: pl.BlockSpec(
        (None, None, None, 1, D_MODEL), lambda m: (layer, _cond_row(tok_off + m * tm), which, 0, 0))
    return pl.pallas_call(
        _norm_mod_kernel,
        grid=(n // tm,),
        in_specs=[pl.BlockSpec((tm, D_MODEL), lambda m: (m, 0)),
                  pl.BlockSpec((None, 1, D_MODEL), lambda m: (layer, 0, 0)),
                  mod_spec(which_shift), mod_spec(which_scale)],
        out_specs=pl.BlockSpec((tm, D_MODEL), lambda m: (m, 0)),
        out_shape=jax.ShapeDtypeStruct((n, D_MODEL), BF16),
        compiler_params=_params(1),
        name="norm_mod",
    )(x, gain.reshape(-1, 1, D_MODEL), modt, modt)


def _final_norm_kernel(x_ref, g_ref, o_ref):
    x = x_ref[...]
    ms = jnp.mean(x * x, axis=-1, keepdims=True)
    o_ref[...] = x * lax.rsqrt(ms + NORM_EPS) * g_ref[...]


def final_rmsnorm(x, gain, row_off, rows, tm=1024):
    off = row_off // tm
    return pl.pallas_call(
        _final_norm_kernel,
        grid=(rows // tm,),
        in_specs=[pl.BlockSpec((tm, D_MODEL), lambda m: (off + m, 0)),
                  pl.BlockSpec((1, D_MODEL), lambda m: (0, 0))],
        out_specs=pl.BlockSpec((tm, D_MODEL), lambda m: (m, 0)),
        out_shape=jax.ShapeDtypeStruct((rows, D_MODEL), F32),
        compiler_params=_params(1),
        name="final_norm",
    )(x, gain.reshape(1, D_MODEL))


def _mm_kernel(*refs, nk, epilogue, scale, n_a, n_x, split):
    refs = list(refs)
    a_refs = [refs.pop(0) for _ in range(n_a)]
    w_ref = refs.pop(0)
    x_refs = [refs.pop(0) for _ in range(n_x)]
    gate_ref = refs.pop(0) if epilogue == "resid" else None
    o_ref = refs.pop(0)
    wb_ref = refs.pop(0)
    acc_ref = refs.pop(0) if nk > 1 else None
    m = pl.program_id(1)
    k = pl.program_id(2)

    def pick(parts):
        if len(parts) == 1:
            return parts[0][...]
        return jnp.where(m < split, parts[0][...], parts[1][...])

    @pl.when(m == 0)
    def _():
        wb_ref[k] = w_ref[...].astype(BF16)

    part = jnp.dot(pick(a_refs), wb_ref[k], preferred_element_type=F32)

    def finish(acc):
        if epilogue == "cast":
            if scale is not None:
                acc = jnp.where(pl.program_id(0) < scale[1], acc * scale[0], acc)
            o_ref[...] = acc.astype(o_ref.dtype)
        elif epilogue == "relu2":
            r2 = jnp.maximum(acc, 0.0)
            o_ref[...] = (r2 * r2).astype(o_ref.dtype)
        else:
            o_ref[...] = pick(x_refs) + gate_ref[...] * acc

    if nk == 1:
        finish(part)
    else:
        @pl.when(k == 0)
        def _():
            acc_ref[...] = part

        @pl.when(jnp.logical_and(k > 0, k < nk - 1))
        def _():
            acc_ref[...] += part

        @pl.when(k == nk - 1)
        def _():
            finish(acc_ref[...] + part)


def matmul(a, w, *, layer, col_off, cols, tn, out_dtype, tm=1024, tk=2048, row_off=0, rows=None,
           epilogue="cast", scale=None, x=None, modt=None, mod_layer=None, which=None, k_off=0, k_len=None):
    a_parts = a if isinstance(a, tuple) else (a,)
    x_parts = () if x is None else (x if isinstance(x, tuple) else (x,))
    kdim = a_parts[0].shape[1] if k_len is None else k_len
    if rows is None:
        rows = sum(p.shape[0] for p in a_parts) - row_off
    nk, nn, nm = kdim // tk, cols // tn, rows // tm
    m_off, n_off, kb_off = row_off // tm, col_off // tn, k_off // tk
    split = N_CTX // tm
    cond_row = lambda m: _cond_row((m_off + m) * tm)

    def row_specs(parts, width, col_index):
        if len(parts) == 1:
            return [pl.BlockSpec((tm, width), lambda n, m, k: (m_off + m, col_index(n, m, k, True)))]
        return [pl.BlockSpec((tm, width), lambda n, m, k: (jnp.minimum(m, split - 1), col_index(n, m, k, m < split))),
                pl.BlockSpec((tm, width), lambda n, m, k: (jnp.maximum(m - split, 0), col_index(n, m, k, m >= split)))]

    in_specs = row_specs(a_parts, tk, lambda n, m, k, live: kb_off + jnp.where(live, k, 0))
    args = list(a_parts)
    in_specs.append(pl.BlockSpec(
        (None, tk, tn), lambda n, m, k: (layer, kb_off + jnp.where(m == 0, k, nk - 1), n_off + n)))
    args.append(w)
    if epilogue == "resid":
        in_specs += row_specs(x_parts, tn, lambda n, m, k, live: n)
        in_specs.append(pl.BlockSpec((None, None, None, 1, tn),
                                     lambda n, m, k: (mod_layer, cond_row(m), which, 0, n)))
        args += list(x_parts) + [modt]
    scratch = [pltpu.VMEM((nk, tk, tn), BF16)]
    if nk > 1:
        scratch.append(pltpu.VMEM((tm, tn), F32))
    return pl.pallas_call(
        functools.partial(_mm_kernel, nk=nk, epilogue=epilogue, scale=scale, n_a=len(a_parts),
                          n_x=len(x_parts), split=split),
        grid=(nn, nm, nk),
        in_specs=in_specs,
        out_specs=pl.BlockSpec((tm, tn), lambda n, m, k: (m, n)),
        out_shape=jax.ShapeDtypeStruct((rows, cols), out_dtype),
        scratch_shapes=scratch,
        compiler_params=_params(3),
        name="matmul_" + epilogue,
    )(*args)


def _ctx_attn_kernel(q_ref, k_ref, v_ref, o_ref):
    for h in range(NA_HEADS):
        sl = slice(h * NA_HEAD_DIM, (h + 1) * NA_HEAD_DIM)
        q = q_ref[:, sl]
        k = k_ref[:, sl].astype(BF16)
        v = v_ref[:, sl].astype(BF16)
        s = lax.dot_general(q, k, (((1,), (1,)), ((), ())), preferred_element_type=F32)
        p = jnp.exp(s - jnp.max(s, axis=-1, keepdims=True))
        l = jnp.sum(p, axis=-1, keepdims=True)
        o = jnp.dot(p.astype(BF16), v, preferred_element_type=F32) / l
        o_ref[:, sl] = o.astype(o_ref.dtype)


def ctx_attention(q, k, v):
    spec = pl.BlockSpec((SEQ, D_MODEL), lambda b: (b, 0))
    return pl.pallas_call(
        _ctx_attn_kernel,
        grid=(BATCH,),
        in_specs=[spec, spec, spec],
        out_specs=spec,
        out_shape=jax.ShapeDtypeStruct((N_CTX, D_MODEL), BF16),
        compiler_params=_params(1),
        name="ctx_attention",
    )(q, k, v)


GRID_ROWS = DEC_SEQ // GRID_W
Q_BLOCK_ROWS = 4
KEY_WIN_ROWS = 12
N_ROW_OFFS = 2 * WIN_ROWS - 1
TILE_BOTH, TILE_SECOND, TILE_FIRST = 0, N_ROW_OFFS - 1, 2 * N_ROW_OFFS - 1
TILE_NONE = 3 * N_ROW_OFFS - 1
N_BIAS_TILES = TILE_NONE + 1


def _row_start(r):
    return min(max(r - WIN_ROWS // 2, 0), GRID_ROWS - WIN_ROWS)


def _key_win_start(rb):
    return min(max(rb * Q_BLOCK_ROWS - WIN_ROWS // 2, 0), GRID_ROWS - KEY_WIN_ROWS)


def _na_bias_table(rpb):
    reach = WIN_COLS - 1
    qc = np.arange(GRID_W)[:, None]
    kc = np.arange(GRID_W)[None, :]
    ws = np.clip(qc - WIN_COLS // 2, 0, GRID_W - WIN_COLS)
    in_window = (kc >= ws) & (kc < ws + WIN_COLS)
    col_off = np.clip(kc - qc, -reach, reach) + reach
    pick = ((np.arange(2 * reach + 1)[:, None, None] == col_off[None]) & in_window[None]).astype(np.float32)
    t = jnp.einsum("hdk,kqc->hdqc", rpb, pick, precision=lax.Precision.HIGHEST)
    t = jnp.where(in_window, t, NEG_INF)
    masked = jnp.full_like(t, NEG_INF)
    return jnp.concatenate([
        jnp.concatenate([t[:, :-1], t[:, 1:]], axis=-1),
        jnp.concatenate([masked, t], axis=-1),
        jnp.concatenate([t, masked], axis=-1),
        jnp.concatenate([masked[:, :1], masked[:, :1]], axis=-1)], axis=1)


def _bias_tile_index(r, key_row):
    rs = _row_start(r)
    first_ok = rs <= key_row < rs + WIN_ROWS
    second_ok = rs <= key_row + 1 < rs + WIN_ROWS
    off = key_row - r + WIN_ROWS - 1
    if first_ok and second_ok:
        return TILE_BOTH + off
    if second_ok:
        return TILE_SECOND + off + 1
    if first_ok:
        return TILE_FIRST + off
    return TILE_NONE


def _na_kernel(q_ref, k_ref, v_ref, ck_ref, cv_ref, t_ref, o_ref):
    ck = ck_ref[...].astype(BF16)
    cv = cv_ref[...].astype(BF16)
    nt = (((1,), (1,)), ((), ()))
    blk = Q_BLOCK_ROWS * GRID_W
    for rb in range(GRID_ROWS // Q_BLOCK_ROWS):
        kw0 = _key_win_start(rb)
        keys = slice(kw0 * GRID_W, (kw0 + KEY_WIN_ROWS) * GRID_W)
        q = q_ref[rb * blk:(rb + 1) * blk, :]
        bias = jnp.concatenate([
            jnp.concatenate([t_ref[_bias_tile_index(rb * Q_BLOCK_ROWS + i, kw0 + 2 * p)]
                             for p in range(KEY_WIN_ROWS // 2)], axis=1)
            for i in range(Q_BLOCK_ROWS)], axis=0)
        s_loc = lax.dot_general(q, k_ref[keys, :], nt, preferred_element_type=F32) + bias
        s_ctx = lax.dot_general(q, ck, nt, preferred_element_type=F32)
        mx = jnp.maximum(jnp.max(s_loc, axis=-1, keepdims=True), jnp.max(s_ctx, axis=-1, keepdims=True))
        p_loc = jnp.exp(s_loc - mx)
        p_ctx = jnp.exp(s_ctx - mx)
        l = jnp.sum(p_loc, axis=-1, keepdims=True) + jnp.sum(p_ctx, axis=-1, keepdims=True)
        o = (jnp.dot(p_loc.astype(BF16), v_ref[keys, :], preferred_element_type=F32)
             + jnp.dot(p_ctx.astype(BF16), cv, preferred_element_type=F32)) / l
        o_ref[rb * blk:(rb + 1) * blk, :] = o.astype(o_ref.dtype)


def neighborhood_attention(qkv, cache_k, cache_v, cache_layer, bias_table):
    tok = lambda part: pl.BlockSpec((DEC_SEQ, NA_HEAD_DIM), lambda h, b: (b, part * NA_HEADS + h))
    cache = pl.BlockSpec((None, PAST_LEN, NA_HEAD_DIM), lambda h, b: (b, cache_layer, h))
    return pl.pallas_call(
        _na_kernel,
        grid=(NA_HEADS, DEC_BATCH),
        in_specs=[tok(0), tok(1), tok(2), cache, cache,
                  pl.BlockSpec((None, N_BIAS_TILES, GRID_W, 2 * GRID_W), lambda h, b: (h, 0, 0, 0))],
        out_specs=tok(0),
        out_shape=jax.ShapeDtypeStruct((N_LAT, D_MODEL), BF16),
        compiler_params=_params(2),
        name="neighborhood_attention",
    )(qkv, qkv, qkv, cache_k, cache_v, bias_table)


N_SSD_INPUTS = 13


def _ssd_kernel(*refs, seq_len, has_h0, emit_state, seqs_per_step):
    n_in = N_SSD_INPUTS + (1 if has_h0 else 0)
    n_out = 1 + (1 if emit_state else 0)
    ins, outs, scratch = refs[:n_in], refs[n_in:n_in + n_out], refs[n_in + n_out:]
    nc = seq_len // SSM_CHUNK
    for s in range(seqs_per_step):
        tok = pl.ds(s * seq_len, seq_len)
        views = [ins[0].at[tok], ins[1].at[tok], ins[2].at[tok]] + list(ins[3:9])
        views += [ins[9].at[pl.ds(s * nc, nc)]] + list(ins[10:N_SSD_INPUTS])
        if has_h0:
            views.append(ins[N_SSD_INPUTS].at[s])
        views.append(outs[0].at[tok])
        if emit_state:
            views.append(outs[1].at[s])
        views += [r.at[s] for r in scratch]
        _ssd_sequence(*views, seq_len=seq_len, has_h0=has_h0, emit_state=emit_state)


def _ssd_sequence(*refs, seq_len, has_h0, emit_state):
    (x_ref, b_ref, c_ref, cwx_ref, cwb_ref, cwc_ref, cbx_ref, cbb_ref, cbc_ref,
     dt_ref, bias_ref, alog_ref, dsk_ref) = refs[:N_SSD_INPUTS]
    rest = list(refs[N_SSD_INPUTS:])
    h0_ref = rest.pop(0) if has_h0 else None
    y_ref = rest.pop(0)
    st_ref = rest.pop(0) if emit_state else None
    (xs_ref, bt_ref, cs_ref, act3_ref, rowp_ref, rowd_ref, cdb_ref, wr_ref, sloc_ref, hprev_ref, hcur_ref) = rest

    q = SSM_CHUNK
    nc = seq_len // q
    hp = lax.Precision.HIGHEST
    e_n = HEADS_PER_GROUP
    half = SSM_HEAD_DIM
    n_pairs = e_n // 2

    li = lax.broadcasted_iota(jnp.int32, (q, q), 0)
    si = lax.broadcasted_iota(jnp.int32, (q, q), 1)
    tril = (si <= li).astype(F32)
    triu = (si >= li).astype(F32)
    lower = si <= li
    diag = si == li
    first_half = si < half
    fwd_row = lax.broadcasted_iota(jnp.int32, (2 * e_n, q), 0) < e_n
    a_hm = -jnp.exp(alog_ref[...]) * LOG2E

    def chunk_rows(c):
        return pl.ds(c * q, q) if isinstance(c, int) else pl.ds(pl.multiple_of(c * q, q), q)

    def block_diag(pair):
        zero = jnp.zeros_like(pair)
        return jnp.concatenate([jnp.where(first_half, pair, zero), jnp.where(first_half, zero, pair)], axis=0)

    dt_all = _softplus(dt_ref[...] + bias_ref[...])
    da_all = (dt_all * a_hm).reshape(nc * 2 * e_n, q)
    fwd_rows = (lax.broadcasted_iota(jnp.int32, (nc * 2 * e_n, q), 0) % (2 * e_n)) < e_n
    ac_all = jnp.where(fwd_rows,
                       jnp.dot(da_all, triu, precision=hp, preferred_element_type=F32),
                       jnp.dot(da_all, tril, precision=hp, preferred_element_type=F32)).reshape(nc, 2 * e_n, q)
    rowp_ref[...] = ac_all - jnp.log2(dt_all)
    rowd_ref[...] = jnp.log2(dt_all[:, :e_n, :] + dt_all[:, e_n:, :])
    ends = jnp.broadcast_to(jnp.where(fwd_row[None, :, 0:1], ac_all[:, :, q - 1:q], ac_all[:, :, 0:1]), ac_all.shape)
    wr_ref[...] = jnp.exp2(ends - ac_all) * dt_all
    cdb_ref[...] = jnp.exp2(ends)
    hi = ac_all.astype(BF16).astype(F32)
    mid = (ac_all - hi).astype(BF16).astype(F32)
    lo = (ac_all - hi - mid).astype(BF16).astype(F32)
    for c in range(nc):
        act3_ref[c * q:(c + 1) * q, :] = jnp.concatenate([hi[c], mid[c], lo[c]], axis=0).T.astype(BF16)

    def prep(c, carry):
        rows = chunk_rows(c)
        r0 = c * q

        def conv_silu(u_ref, w_ref, bias_r):
            u = u_ref[rows, :].astype(F32)
            width = u.shape[1]
            before = u_ref[pl.ds(pl.multiple_of(jnp.maximum(r0 - 16, 0), 16), 16), :][15:16, :].astype(F32)
            after = u_ref[pl.ds(pl.multiple_of(jnp.minimum(r0 + q, seq_len - 16), 16), 16), :][0:1, :].astype(F32)
            before = jnp.where(c > 0, before, 0.0)
            after = jnp.where(c < nc - 1, after, 0.0)
            rid = lax.broadcasted_iota(jnp.int32, (q, width), 0)
            prev = jnp.where(rid == 0, before, pltpu.roll(u, 1, axis=0))
            nxt = jnp.where(rid == q - 1, after, pltpu.roll(u, q - 1, axis=0))
            w = w_ref[...] * 0.5
            h = prev * w[0:1] + u * w[1:2] + nxt * w[2:3] + bias_r[...] * 0.5
            return h + h * jnp.tanh(h)

        xc = conv_silu(x_ref, cwx_ref, cbx_ref).astype(BF16)
        xs_ref[rows, :] = xc
        cs_ref[rows, :] = conv_silu(c_ref, cwc_ref, cbc_ref).astype(BF16)
        bt = conv_silu(b_ref, cwb_ref, cbb_ref).T
        bt_ref[c] = bt
        wr = wr_ref[c]

        for j in range(n_pairs):
            lanes = slice(j * 2 * half, (j + 1) * 2 * half)
            xblk = block_diag(xc[:, lanes])
            e1, e2 = 2 * j, 2 * j + 1
            scaled = lambda r: (bt * wr[r:r + 1, :]).astype(BF16)
            lhs = jnp.concatenate([
                jnp.concatenate([scaled(e1), scaled(e2)], axis=1),
                jnp.concatenate([scaled(e_n + e1), scaled(e_n + e2)], axis=1)], axis=0)
            out = jnp.dot(lhs, xblk, preferred_element_type=F32)
            sloc_ref[c, 0, :, lanes] = out[:q]
            sloc_ref[c, 1, :, lanes] = out[q:]
        return carry

    unrolled = nc <= 2
    if unrolled:
        for c in range(nc):
            prep(c, 0)
    else:
        lax.fori_loop(0, nc, prep, 0, unroll=2)

    if has_h0:
        hcur_ref[0] = h0_ref[0].T
        hcur_ref[1] = h0_ref[1].T
    else:
        hcur_ref[...] = jnp.zeros(hcur_ref.shape, F32)

    def carry_states(direction):
        def body(i, carry):
            c = i if direction == 0 else nc - 1 - i
            h = hcur_ref[direction]
            hprev_ref[c, direction] = h.astype(BF16)
            cd = cdb_ref[c]
            r0 = direction * e_n
            decay = jnp.concatenate(
                [jnp.where(first_half[0:1, :], cd[r0 + 2 * j:r0 + 2 * j + 1, :], cd[r0 + 2 * j + 1:r0 + 2 * j + 2, :])
                 for j in range(n_pairs)], axis=1)
            hcur_ref[direction] = h * decay + sloc_ref[c, direction]
            return carry
        if unrolled:
            for i in range(nc):
                body(i, 0)
        else:
            lax.fori_loop(0, nc, body, 0)

    carry_states(0)
    carry_states(1)
    if emit_state:
        st_ref[0] = hcur_ref[0].T
        st_ref[1] = hcur_ref[1].T

    dsum = dsk_ref[0:1, :] + dsk_ref[1:2, :]
    sel_k = lax.broadcasted_iota(jnp.int32, (3 * 2 * e_n, 4 * q), 0) % (2 * e_n)
    sel_n = lax.broadcasted_iota(jnp.int32, (3 * 2 * e_n, 4 * q), 1) // q
    spread_sel = [(sel_k == 2 * j + (sel_n % 2) + e_n * (sel_n // 2)).astype(BF16) for j in range(n_pairs)]

    def emit(c, zero_in_f=False, zero_in_b=False):
        rows = chunk_rows(c)
        cc = cs_ref[rows, :]
        cb = jnp.dot(cc, bt_ref[c].astype(BF16), preferred_element_type=F32)
        off_f = None if zero_in_f else jnp.dot(cc, hprev_ref[c, 0], preferred_element_type=F32)
        off_b = None if zero_in_b else jnp.dot(cc, hprev_ref[c, 1], preferred_element_type=F32)
        act3 = act3_ref[rows, :]
        rp = rowp_ref[c]
        rd = rowd_ref[c]
        spreads = [jnp.dot(act3, spread_sel[j], preferred_element_type=F32) for j in range(n_pairs)]
        for j in range(n_pairs):
            lanes = slice(j * 2 * half, (j + 1) * 2 * half)
            spread = spreads[j]
            acf = [spread[:, 0:q], spread[:, q:2 * q]]
            acb = [spread[:, 2 * q:3 * q], spread[:, 3 * q:4 * q]]
            mix = []
            for i, e in enumerate((2 * j, 2 * j + 1)):
                arg = jnp.where(diag, rd[e:e + 1, :],
                                jnp.where(lower, acf[i] - rp[e:e + 1, :], acb[i] - rp[e_n + e:e_n + e + 1, :]))
                mix.append((cb * jnp.exp2(arg)).astype(BF16))
            xp = xs_ref[rows, lanes]
            y = jnp.dot(jnp.concatenate(mix, axis=1), block_diag(xp), preferred_element_type=F32)
            if off_f is not None:
                y = y + off_f[:, lanes] * jnp.exp2(jnp.where(first_half, acf[0], acf[1]))
            if off_b is not None:
                y = y + off_b[:, lanes] * jnp.exp2(jnp.where(first_half, acb[0], acb[1]))
            y_ref[rows, lanes] = (y + xp.astype(F32) * dsum[:, lanes]).astype(y_ref.dtype)

    if has_h0:
        def emit_body(c, carry):
            emit(c)
            return carry
        lax.fori_loop(0, nc, emit_body, 0, unroll=2)
    else:
        for c in range(nc):
            emit(c, zero_in_f=(c == 0), zero_in_b=(c == nc - 1))


def ssd_scan(proj, dt_hm, conv_w, conv_b, bias_hm, alog_hm, d_skip_wide, *, seq_len, n_seq, row_off,
             seqs_per_step=1, h0=None):
    has_h0 = h0 is not None
    emit_state = not has_h0
    nc = seq_len // SSM_CHUNK
    sps = seqs_per_step
    s_off = row_off // (sps * seq_len)
    gw = GROUP_WIDTH
    ns = SSM_STATE
    e2 = 2 * HEADS_PER_GROUP
    x_blk0 = SSM_D_INNER // gw
    b_blk0 = 2 * SSM_D_INNER // ns
    c_blk0 = b_blk0 + SSM_GROUPS
    cw_b0 = SSM_D_INNER // ns
    in_specs = [
        pl.BlockSpec((sps * seq_len, gw), lambda s, g: (s_off + s, x_blk0 + g)),
        pl.BlockSpec((sps * seq_len, ns), lambda s, g: (s_off + s, b_blk0 + g)),
        pl.BlockSpec((sps * seq_len, ns), lambda s, g: (s_off + s, c_blk0 + g)),
        pl.BlockSpec((None, 3, gw), lambda s, g: (0, 0, g)),
        pl.BlockSpec((None, 3, ns), lambda s, g: (0, 0, cw_b0 + g)),
        pl.BlockSpec((None, 3, ns), lambda s, g: (0, 0, cw_b0 + SSM_GROUPS + g)),
        pl.BlockSpec((None, 1, gw), lambda s, g: (0, 0, g)),
        pl.BlockSpec((None, 1, ns), lambda s, g: (0, 0, cw_b0 + g)),
        pl.BlockSpec((None, 1, ns), lambda s, g: (0, 0, cw_b0 + SSM_GROUPS + g)),
        pl.BlockSpec((None, sps * nc, e2, SSM_CHUNK), lambda s, g: (g, s_off + s, 0, 0)),
        pl.BlockSpec((None, e2, 1), lambda s, g: (g, 0, 0)),
        pl.BlockSpec((None, e2, 1), lambda s, g: (g, 0, 0)),
        pl.BlockSpec((2, gw), lambda s, g: (0, g)),
    ]
    args = [proj, proj, proj, conv_w, conv_w, conv_w, conv_b, conv_b, conv_b,
            dt_hm, bias_hm, alog_hm, d_skip_wide]
    assert len(args) == N_SSD_INPUTS
    state_spec = pl.BlockSpec((sps, 2, gw, ns), lambda s, g: (s, 0, g, 0))
    if has_h0:
        in_specs.append(state_spec)
        args.append(h0)
    y_spec = pl.BlockSpec((sps * seq_len, gw), lambda s, g: (s, g))
    y_shape = jax.ShapeDtypeStruct((n_seq * seq_len, SSM_D_INNER), BF16)
    if emit_state:
        out_specs = [y_spec, state_spec]
        out_shape = [y_shape, jax.ShapeDtypeStruct((n_seq, 2, SSM_D_INNER, ns), F32)]
    else:
        out_specs = y_spec
        out_shape = y_shape
    return pl.pallas_call(
        functools.partial(_ssd_kernel, seq_len=seq_len, has_h0=has_h0, emit_state=emit_state,
                          seqs_per_step=sps),
        grid=(n_seq // sps, SSM_GROUPS),
        in_specs=in_specs,
        out_specs=out_specs,
        out_shape=out_shape,
        scratch_shapes=[
            pltpu.VMEM((sps, seq_len, gw), BF16),
            pltpu.VMEM((sps, nc, ns, SSM_CHUNK), F32),
            pltpu.VMEM((sps, seq_len, ns), BF16),
            pltpu.VMEM((sps, seq_len, 3 * e2), BF16),
            pltpu.VMEM((sps, nc, e2, SSM_CHUNK), F32),
            pltpu.VMEM((sps, nc, HEADS_PER_GROUP, SSM_CHUNK), F32),
            pltpu.VMEM((sps, nc, e2, SSM_CHUNK), F32),
            pltpu.VMEM((sps, nc, e2, SSM_CHUNK), F32),
            pltpu.VMEM((sps, nc, 2, ns, gw), F32),
            pltpu.VMEM((sps, nc, 2, ns, gw), BF16),
            pltpu.VMEM((sps, 2, ns, gw), F32)],
        compiler_params=_params(2),
        name="ssd_scan",
    )(*args)


def _gate_norm_kernel(y_ctx_ref, y_lat_ref, z_ref, g_ref, o_ref, *, split):
    y = jnp.where(pl.program_id(0) < split, y_ctx_ref[...], y_lat_ref[...])
    v = y.astype(F32) * _silu(z_ref[...].astype(F32))
    ms = jnp.mean(v * v, axis=-1, keepdims=True)
    o_ref[...] = (v * lax.rsqrt(ms + NORM_EPS) * g_ref[...]).astype(o_ref.dtype)


def gate_norm(y_ctx, y_lat, proj, gain, tm=512):
    split = N_CTX // tm
    wide = lambda index: pl.BlockSpec((tm, SSM_D_INNER), index)
    return pl.pallas_call(
        functools.partial(_gate_norm_kernel, split=split),
        grid=(N_TOK // tm,),
        in_specs=[wide(lambda m: (jnp.minimum(m, split - 1), 0)),
                  wide(lambda m: (jnp.maximum(m - split, 0), 0)),
                  wide(lambda m: (m, 0)),
                  pl.BlockSpec((1, SSM_D_INNER), lambda m: (0, 0))],
        out_specs=wide(lambda m: (m, 0)),
        out_shape=jax.ShapeDtypeStruct((N_TOK, SSM_D_INNER), BF16),
        compiler_params=_params(1),
        name="gate_norm",
    )(y_ctx, y_lat, proj, gain.reshape(1, SSM_D_INNER))


def _mlp(x, modt, layer, norm_mlp, mlp_w1, mlp_w2):
    h = norm_mod(x, norm_mlp, modt, layer, 3, 4)
    a = matmul(h, mlp_w1, layer=layer, col_off=0, cols=D_FF, tn=1024, tm=2048, out_dtype=BF16, epilogue="relu2")
    half = D_FF // 2
    for k_off in (0, half):
        x = matmul(a, mlp_w2, layer=layer, col_off=0, cols=D_MODEL, tn=512, tk=half, k_off=k_off, k_len=half,
                   out_dtype=F32, epilogue="resid", x=x, modt=modt, mod_layer=layer, which=5)
    return x


def kernel(x_prompt, x_sample, cache_attn_k, cache_attn_v, state_ssm, c, c_ctx, mod_w, mod_b, norm_mix, norm_mlp, mlp_w1, mlp_w2, na_wqkv, na_wo, na_rpb, ssm_w_in, ssm_conv_w, ssm_conv_b, ssm_dt_bias, ssm_a_log, ssm_d, ssm_norm, ssm_w_out, final_norm):
    x_ctx = x_prompt.reshape(N_CTX, D_MODEL)
    x_lat = x_sample.reshape(N_LAT, D_MODEL)
    cond = jnp.concatenate([c_ctx[None, :], c, jnp.zeros((COND_ROWS - 1 - DEC_BATCH, D_MODEL), F32)], axis=0)
    modt = mod_table(cond, mod_w, mod_b)

    h_ctx = norm_mod(x_ctx, norm_mix, modt, 0, 0, 1)
    h_lat = norm_mod(x_lat, norm_mix, modt, 0, 0, 1, tok_off=N_CTX)
    qkv = functools.partial(matmul, w=na_wqkv, layer=0, cols=D_MODEL, tn=1024)
    q_scale = (NA_HEAD_DIM ** -0.5, D_MODEL // 1024)
    q_ctx = qkv(h_ctx, col_off=0, out_dtype=BF16, scale=q_scale, tm=2048)
    k_ctx = qkv(h_ctx, col_off=D_MODEL, out_dtype=F32)
    v_ctx = qkv(h_ctx, col_off=2 * D_MODEL, out_dtype=F32)
    qkv_lat = matmul(h_lat, na_wqkv, layer=0, col_off=0, cols=3 * D_MODEL, tn=1024, tm=2048, out_dtype=BF16,
                     scale=q_scale)
    o_ctx = ctx_attention(q_ctx, k_ctx, v_ctx)
    n_attn = cache_attn_k.shape[1]
    o_lat = neighborhood_attention(
        qkv_lat,
        cache_attn_k.reshape(DEC_BATCH, n_attn * PAST_LEN, D_MODEL),
        cache_attn_v.reshape(DEC_BATCH, n_attn * PAST_LEN, D_MODEL),
        0, _na_bias_table(na_rpb[0]))
    x = matmul((o_ctx, o_lat), na_wo, layer=0, col_off=0, cols=D_MODEL, tn=1024, tm=512, out_dtype=F32,
               epilogue="resid", x=(x_ctx, x_lat), modt=modt, mod_layer=0, which=2)
    x = _mlp(x, modt, 0, norm_mlp, mlp_w1, mlp_w2)

    h = norm_mod(x, norm_mix, modt, 1, 0, 1)
    zxbc_cols = SSM_D_INNER + SSM_CONV_DIM
    proj = matmul(h, ssm_w_in, layer=0, col_off=0, cols=zxbc_cols, tn=1024, tm=2048, out_dtype=BF16)
    dt_raw = matmul(h, ssm_w_in, layer=0, col_off=zxbc_cols, cols=2 * SSM_HEADS, tn=128, out_dtype=F32)
    e_n = HEADS_PER_GROUP
    dt_hm = dt_raw.reshape(N_TOK // SSM_CHUNK, SSM_CHUNK, 2, SSM_GROUPS, e_n).transpose(3, 0, 2, 4, 1)
    dt_hm = dt_hm.reshape(SSM_GROUPS, N_TOK // SSM_CHUNK, 2 * e_n, SSM_CHUNK)
    by_group = lambda p: p.reshape(2, SSM_GROUPS, e_n).transpose(1, 0, 2).reshape(SSM_GROUPS, 2 * e_n, 1)
    n_ssm = state_ssm.shape[1]
    ssd = functools.partial(
        ssd_scan, proj, dt_hm, ssm_conv_w, ssm_conv_b.reshape(1, 1, SSM_CONV_DIM),
        by_group(ssm_dt_bias[0]), by_group(ssm_a_log[0]), jnp.repeat(ssm_d[0], SSM_HEAD_DIM, axis=1))
    y_ctx, new_state = ssd(seq_len=SEQ, n_seq=BATCH, row_off=0, seqs_per_step=4)
    y_lat = ssd(seq_len=DEC_SEQ, n_seq=DEC_BATCH, row_off=N_CTX,
                h0=state_ssm.reshape(DEC_BATCH, n_ssm * 2, SSM_D_INNER, SSM_STATE))
    yn = gate_norm(y_ctx, y_lat, proj, ssm_norm)
    x = matmul(yn, ssm_w_out, layer=0, col_off=0, cols=D_MODEL, tn=512, tk=SSM_D_INNER, out_dtype=F32,
               epilogue="resid", x=x, modt=modt, mod_layer=1, which=2)
    x = _mlp(x, modt, 1, norm_mlp, mlp_w1, mlp_w2)

    y_prompt = final_rmsnorm(x, final_norm, 0, N_CTX).reshape(BATCH, SEQ, D_MODEL)
    y_sample = final_rmsnorm(x, final_norm, N_CTX, N_LAT).reshape(DEC_BATCH, DEC_SEQ, D_MODEL)
    kv_shape = (BATCH, 1, SEQ, NA_HEADS, NA_HEAD_DIM)
    return (y_prompt, y_sample, k_ctx.reshape(kv_shape), v_ctx.reshape(kv_shape),
            new_state.reshape(BATCH, 1, 2, SSM_HEADS, SSM_HEAD_DIM, SSM_STATE))
```

```python
import functools

import jax
import jax.numpy as jnp
import numpy as np
from jax import lax
from jax.experimental import pallas as pl
from jax.experimental.pallas import tpu as pltpu

F32 = jnp.float32
BF16 = jnp.bfloat16

D_MODEL = 2048
BATCH = 32
SEQ = 256
DEPTH = 2
DEC_BATCH = 4
DEC_SEQ = 1024
PAST_LEN = 256
GRID_W = 64
NA_HEADS = 16
NA_HEAD_DIM = 128
WIN_ROWS = 8
WIN_COLS = 16
SSM_D_INNER = 4096
SSM_HEAD_DIM = 64
SSM_HEADS = 64
SSM_GROUPS = 8
SSM_STATE = 128
SSM_CHUNK = 128
SSM_CONV_DIM = SSM_D_INNER + 2 * SSM_GROUPS * SSM_STATE
D_FF = 4 * D_MODEL
NORM_EPS = 1e-6
NEG_INF = -1e30
LOG2E = 1.4426950408889634

N_CTX = BATCH * SEQ
N_LAT = DEC_BATCH * DEC_SEQ
N_TOK = N_CTX + N_LAT
COND_ROWS = 8
HEADS_PER_GROUP = SSM_HEADS // SSM_GROUPS
GROUP_WIDTH = HEADS_PER_GROUP * SSM_HEAD_DIM

VMEM_LIMIT_BYTES = 56 * 1024 * 1024


def _params(n_axes):
    return pltpu.CompilerParams(dimension_semantics=("arbitrary",) * n_axes,
                                vmem_limit_bytes=VMEM_LIMIT_BYTES)


def _silu(v):
    h = 0.5 * v
    return h + h * jnp.tanh(h)


def _softplus(x):
    return jnp.maximum(x, 0.0) + jnp.log(1.0 + jnp.exp(-jnp.abs(x)))


def _cond_row(row_start):
    return jnp.where(row_start < N_CTX, 0, 1 + (row_start - N_CTX) // DEC_SEQ)


def _mod_kernel(cond_ref, w_ref, b_ref, o_ref):
    s = _silu(cond_ref[...]).astype(BF16)
    part = jnp.dot(s, w_ref[...].astype(BF16), preferred_element_type=F32)

    @pl.when(pl.program_id(1) == 0)
    def _():
        o_ref[...] = part + b_ref[...]

    @pl.when(pl.program_id(1) > 0)
    def _():
        o_ref[...] += part


def mod_table(cond, mod_w, mod_b):
    tk = 256
    n6 = 6 * D_MODEL
    out = pl.pallas_call(
        _mod_kernel,
        grid=(DEPTH, D_MODEL // tk),
        in_specs=[pl.BlockSpec((COND_ROWS, tk), lambda l, k: (0, k)),
                  pl.BlockSpec((None, tk, n6), lambda l, k: (l, k, 0)),
                  pl.BlockSpec((None, 1, n6), lambda l, k: (l, 0, 0))],
        out_specs=pl.BlockSpec((None, COND_ROWS, n6), lambda l, k: (l, 0, 0)),
        out_shape=jax.ShapeDtypeStruct((DEPTH, COND_ROWS, n6), F32),
        compiler_params=_params(2),
        name="mod_table",
    )(cond, mod_w, mod_b.reshape(DEPTH, 1, n6))
    return out.reshape(DEPTH, COND_ROWS, 6, 1, D_MODEL)


def _norm_mod_kernel(x_ref, g_ref, shift_ref, scale_ref, o_ref):
    x = x_ref[...]
    ms = jnp.mean(x * x, axis=-1, keepdims=True)
    y = x * lax.rsqrt(ms + NORM_EPS) * g_ref[...]
    o_ref[...] = (y * (1.0 + scale_ref[...]) + shift_ref[...]).astype(o_ref.dtype)


def norm_mod(x, gain, modt, layer, which_shift, which_scale, tok_off=0, tm=1024):
    n = x.shape[0]
    mod_spec = lambda which: pl.BlockSpec(
        (None, None, None, 1, D_MODEL), lambda m: (layer, _cond_row(tok_off + m * tm), which, 0, 0))
    return pl.pallas_call(
        _norm_mod_kernel,
        grid=(n // tm,),
        in_specs=[pl.BlockSpec((tm, D_MODEL), lambda m: (m, 0)),
                  pl.BlockSpec((None, 1, D_MODEL), lambda m: (layer, 0, 0)),
                  mod_spec(which_shift), mod_spec(which_scale)],
        out_specs=pl.BlockSpec((tm, D_MODEL), lambda m: (m, 0)),
        out_shape=jax.ShapeDtypeStruct((n, D_MODEL), BF16),
        compiler_params=_params(1),
        name="norm_mod",
    )(x, gain.reshape(-1, 1, D_MODEL), modt, modt)


def _final_norm_kernel(x_ref, g_ref, o_ref):
    x = x_ref[...]
    ms = jnp.mean(x * x, axis=-1, keepdims=True)
    o_ref[...] = x * lax.rsqrt(ms + NORM_EPS) * g_ref[...]


def final_rmsnorm(x, gain, row_off, rows, tm=1024):
    off = row_off // tm
    return pl.pallas_call(
        _final_norm_kernel,
        grid=(rows // tm,),
        in_specs=[pl.BlockSpec((tm, D_MODEL), lambda m: (off + m, 0)),
                  pl.BlockSpec((1, D_MODEL), lambda m: (0, 0))],
        out_specs=pl.BlockSpec((tm, D_MODEL), lambda m: (m, 0)),
        out_shape=jax.ShapeDtypeStruct((rows, D_MODEL), F32),
        compiler_params=_params(1),
        name="final_norm",
    )(x, gain.reshape(1, D_MODEL))


def _mm_kernel(*refs, epilogue, scale, n_src, split):
    refs = list(refs)
    a_refs = [refs.pop(0) for _ in range(n_src)]
    w_ref = refs.pop(0)
    resid = epilogue == "resid"
    x_refs = [refs.pop(0) for _ in range(n_src)] if resid else [None] * n_src
    gate_ref = refs.pop(0) if resid else None
    o_ref, wb_ref = refs
    m = pl.program_id(1)

    @pl.when(m == 0)
    def _():
        wb_ref[...] = w_ref[...].astype(BF16)

    def step(a_ref, x_ref):
        acc = jnp.dot(a_ref[...], wb_ref[...], preferred_element_type=F32)
        if epilogue == "cast":
            if scale is not None:
                acc = jnp.where(pl.program_id(0) < scale[1], acc * scale[0], acc)
            o_ref[...] = acc.astype(o_ref.dtype)
        elif epilogue == "relu2":
            r = jnp.maximum(acc, 0.0)
            o_ref[...] = (r * r).astype(o_ref.dtype)
        else:
            o_ref[...] = x_ref[...] + gate_ref[...] * acc

    if n_src == 1:
        step(a_refs[0], x_refs[0])
    else:
        pl.when(m < split)(lambda: step(a_refs[0], x_refs[0]))
        pl.when(m >= split)(lambda: step(a_refs[1], x_refs[1]))


def matmul(a, w, *, layer, col_off, cols, tn, out_dtype, tm=1024, k_off=0, k_len=None, epilogue="cast",
           scale=None, x=None, modt=None, mod_layer=None, which=None):
    a_parts = a if isinstance(a, tuple) else (a,)
    x_parts = () if x is None else (x if isinstance(x, tuple) else (x,))
    tk = a_parts[0].shape[1] if k_len is None else k_len
    rows = sum(p.shape[0] for p in a_parts)
    n_off, kb_off = col_off // tn, k_off // tk
    split = N_CTX // tm

    def row_specs(parts, width, col_block):
        if len(parts) == 1:
            return [pl.BlockSpec((tm, width), lambda n, m: (m, col_block(n)))]
        return [pl.BlockSpec((tm, width), lambda n, m: (jnp.minimum(m, split - 1), col_block(n))),
                pl.BlockSpec((tm, width), lambda n, m: (jnp.maximum(m - split, 0), col_block(n)))]

    in_specs = row_specs(a_parts, tk, lambda n: kb_off)
    in_specs.append(pl.BlockSpec((None, tk, tn), lambda n, m: (layer, kb_off, n_off + n)))
    args = list(a_parts) + [w]
    if epilogue == "resid":
        assert len(x_parts) == len(a_parts)
        in_specs += row_specs(x_parts, tn, lambda n: n)
        in_specs.append(pl.BlockSpec((None, None, None, 1, tn),
                                     lambda n, m: (mod_layer, _cond_row(m * tm), which, 0, n)))
        args += list(x_parts) + [modt]
    return pl.pallas_call(
        functools.partial(_mm_kernel, epilogue=epilogue, scale=scale, n_src=len(a_parts), split=split),
        grid=(cols // tn, rows // tm),
        in_specs=in_specs,
        out_specs=pl.BlockSpec((tm, tn), lambda n, m: (m, n)),
        out_shape=jax.ShapeDtypeStruct((rows, cols), out_dtype),
        scratch_shapes=[pltpu.VMEM((tk, tn), BF16)],
        compiler_params=_params(2),
        name="matmul_" + epilogue,
    )(*args)


def _ctx_attn_kernel(q_ref, k_ref, v_ref, o_ref):
    for h in range(NA_HEADS):
        sl = slice(h * NA_HEAD_DIM, (h + 1) * NA_HEAD_DIM)
        q = q_ref[:, sl]
        k = k_ref[:, sl].astype(BF16)
        v = v_ref[:, sl].astype(BF16)
        s = lax.dot_general(q, k, (((1,), (1,)), ((), ())), preferred_element_type=F32)
        p = jnp.exp(s - jnp.max(s, axis=-1, keepdims=True))
        l = jnp.sum(p, axis=-1, keepdims=True)
        o = jnp.dot(p.astype(BF16), v, preferred_element_type=F32) / l
        o_ref[:, sl] = o.astype(o_ref.dtype)


def ctx_attention(q, k, v):
    spec = pl.BlockSpec((SEQ, D_MODEL), lambda b: (b, 0))
    return pl.pallas_call(
        _ctx_attn_kernel,
        grid=(BATCH,),
        in_specs=[spec, spec, spec],
        out_specs=spec,
        out_shape=jax.ShapeDtypeStruct((N_CTX, D_MODEL), BF16),
        compiler_params=_params(1),
        name="ctx_attention",
    )(q, k, v)


GRID_ROWS = DEC_SEQ // GRID_W
Q_BLOCK_ROWS = 4
KEY_WIN_ROWS = 12
N_ROW_OFFS = 2 * WIN_ROWS - 1
TILE_BOTH, TILE_SECOND, TILE_FIRST = 0, N_ROW_OFFS - 1, 2 * N_ROW_OFFS - 1
TILE_NONE = 3 * N_ROW_OFFS - 1
N_BIAS_TILES = TILE_NONE + 1


def _row_start(r):
    return min(max(r - WIN_ROWS // 2, 0), GRID_ROWS - WIN_ROWS)


def _key_win_start(rb):
    return min(max(rb * Q_BLOCK_ROWS - WIN_ROWS // 2, 0), GRID_ROWS - KEY_WIN_ROWS)


def _na_bias_table(rpb):
    reach = WIN_COLS - 1
    qc = np.arange(GRID_W)[:, None]
    kc = np.arange(GRID_W)[None, :]
    ws = np.clip(qc - WIN_COLS // 2, 0, GRID_W - WIN_COLS)
    in_window = (kc >= ws) & (kc < ws + WIN_COLS)
    col_off = np.clip(kc - qc, -reach, reach) + reach
    pick = ((np.arange(2 * reach + 1)[:, None, None] == col_off[None]) & in_window[None]).astype(np.float32)
    t = jnp.einsum("hdk,kqc->hdqc", rpb, pick, precision=lax.Precision.HIGHEST)
    t = jnp.where(in_window, t, NEG_INF)
    masked = jnp.full_like(t, NEG_INF)
    return jnp.concatenate([
        jnp.concatenate([t[:, :-1], t[:, 1:]], axis=-1),
        jnp.concatenate([masked, t], axis=-1),
        jnp.concatenate([t, masked], axis=-1),
        jnp.concatenate([masked[:, :1], masked[:, :1]], axis=-1)], axis=1)


def _bias_tile_index(r, key_row):
    rs = _row_start(r)
    first_ok = rs <= key_row < rs + WIN_ROWS
    second_ok = rs <= key_row + 1 < rs + WIN_ROWS
    off = key_row - r + WIN_ROWS - 1
    if first_ok and second_ok:
        return TILE_BOTH + off
    if second_ok:
        return TILE_SECOND + off + 1
    if first_ok:
        return TILE_FIRST + off
    return TILE_NONE


def _na_kernel(q_ref, k_ref, v_ref, ck_ref, cv_ref, t_ref, o_ref):
    ck = ck_ref[...].astype(BF16)
    cv = cv_ref[...].astype(BF16)
    nt = (((1,), (1,)), ((), ()))
    blk = Q_BLOCK_ROWS * GRID_W
    for rb in range(GRID_ROWS // Q_BLOCK_ROWS):
        kw0 = _key_win_start(rb)
        keys = slice(kw0 * GRID_W, (kw0 + KEY_WIN_ROWS) * GRID_W)
        q = q_ref[rb * blk:(rb + 1) * blk, :]
        bias = jnp.concatenate([
            jnp.concatenate([t_ref[_bias_tile_index(rb * Q_BLOCK_ROWS + i, kw0 + 2 * p)]
                             for p in range(KEY_WIN_ROWS // 2)], axis=1)
            for i in range(Q_BLOCK_ROWS)], axis=0)
        s_loc = lax.dot_general(q, k_ref[keys, :], nt, preferred_element_type=F32) + bias
        s_ctx = lax.dot_general(q, ck, nt, preferred_element_type=F32)
        mx = jnp.maximum(jnp.max(s_loc, axis=-1, keepdims=True), jnp.max(s_ctx, axis=-1, keepdims=True))
        p_loc = jnp.exp(s_loc - mx)
        p_ctx = jnp.exp(s_ctx - mx)
        l = jnp.sum(p_loc, axis=-1, keepdims=True) + jnp.sum(p_ctx, axis=-1, keepdims=True)
        o = (jnp.dot(p_loc.astype(BF16), v_ref[keys, :], preferred_element_type=F32)
             + jnp.dot(p_ctx.astype(BF16), cv, preferred_element_type=F32)) / l
        o_ref[rb * blk:(rb + 1) * blk, :] = o.astype(o_ref.dtype)


def neighborhood_attention(qkv, cache_k, cache_v, cache_layer, bias_table):
    tok = lambda part: pl.BlockSpec((DEC_SEQ, NA_HEAD_DIM), lambda h, b: (b, part * NA_HEADS + h))
    cache = pl.BlockSpec((None, PAST_LEN, NA_HEAD_DIM), lambda h, b: (b, cache_layer, h))
    return pl.pallas_call(
        _na_kernel,
        grid=(NA_HEADS, DEC_BATCH),
        in_specs=[tok(0), tok(1), tok(2), cache, cache,
                  pl.BlockSpec((None, N_BIAS_TILES, GRID_W, 2 * GRID_W), lambda h, b: (h, 0, 0, 0))],
        out_specs=tok(0),
        out_shape=jax.ShapeDtypeStruct((N_LAT, D_MODEL), BF16),
        compiler_params=_params(2),
        name="neighborhood_attention",
    )(qkv, qkv, qkv, cache_k, cache_v, bias_table)


N_SSD_INPUTS = 13


def _ssd_kernel(*refs, seq_len, has_h0, emit_state, seqs_per_step):
    n_in = N_SSD_INPUTS + (1 if has_h0 else 0)
    n_out = 1 + (1 if emit_state else 0)
    ins, outs, scratch = refs[:n_in], refs[n_in:n_in + n_out], refs[n_in + n_out:]
    nc = seq_len // SSM_CHUNK
    for s in range(seqs_per_step):
        tok = pl.ds(s * seq_len, seq_len)
        views = [ins[0].at[tok], ins[1].at[tok], ins[2].at[tok]] + list(ins[3:9])
        views += [ins[9].at[pl.ds(s * nc, nc)]] + list(ins[10:N_SSD_INPUTS])
        if has_h0:
            views.append(ins[N_SSD_INPUTS].at[s])
        views.append(outs[0].at[tok])
        if emit_state:
            views.append(outs[1].at[s])
        views += [r.at[s] for r in scratch]
        _ssd_sequence(*views, seq_len=seq_len, has_h0=has_h0, emit_state=emit_state)


def _ssd_sequence(*refs, seq_len, has_h0, emit_state):
    (x_ref, b_ref, c_ref, cwx_ref, cwb_ref, cwc_ref, cbx_ref, cbb_ref, cbc_ref,
     dt_ref, bias_ref, alog_ref, dsk_ref) = refs[:N_SSD_INPUTS]
    rest = list(refs[N_SSD_INPUTS:])
    h0_ref = rest.pop(0) if has_h0 else None
    y_ref = rest.pop(0)
    st_ref = rest.pop(0) if emit_state else None
    (xs_ref, bt_ref, cs_ref, act3_ref, rowp_ref, rowd_ref, cdb_ref, wr_ref, sloc_ref, hprev_ref, hcur_ref) = rest

    q = SSM_CHUNK
    nc = seq_len // q
    hp = lax.Precision.HIGHEST
    e_n = HEADS_PER_GROUP
    half = SSM_HEAD_DIM
    n_pairs = e_n // 2

    li = lax.broadcasted_iota(jnp.int32, (q, q), 0)
    si = lax.broadcasted_iota(jnp.int32, (q, q), 1)
    tril = (si <= li).astype(F32)
    triu = (si >= li).astype(F32)
    lower = si <= li
    diag = si == li
    first_half = si < half
    fwd_row = lax.broadcasted_iota(jnp.int32, (2 * e_n, q), 0) < e_n
    a_hm = -jnp.exp(alog_ref[...]) * LOG2E

    def chunk_rows(c):
        return pl.ds(c * q, q) if isinstance(c, int) else pl.ds(pl.multiple_of(c * q, q), q)

    def block_diag(pair):
        zero = jnp.zeros_like(pair)
        return jnp.concatenate([jnp.where(first_half, pair, zero), jnp.where(first_half, zero, pair)], axis=0)

    dt_all = _softplus(dt_ref[...] + bias_ref[...])
    da_all = (dt_all * a_hm).reshape(nc * 2 * e_n, q)
    fwd_rows = (lax.broadcasted_iota(jnp.int32, (nc * 2 * e_n, q), 0) % (2 * e_n)) < e_n
    ac_all = jnp.where(fwd_rows,
                       jnp.dot(da_all, triu, precision=hp, preferred_element_type=F32),
                       jnp.dot(da_all, tril, precision=hp, preferred_element_type=F32)).reshape(nc, 2 * e_n, q)
    rowp_ref[...] = ac_all - jnp.log2(dt_all)
    rowd_ref[...] = jnp.log2(dt_all[:, :e_n, :] + dt_all[:, e_n:, :])
    ends = jnp.broadcast_to(jnp.where(fwd_row[None, :, 0:1], ac_all[:, :, q - 1:q], ac_all[:, :, 0:1]), ac_all.shape)
    wr_ref[...] = jnp.exp2(ends - ac_all) * dt_all
    cdb_ref[...] = jnp.exp2(ends)
    hi = ac_all.astype(BF16).astype(F32)
    mid = (ac_all - hi).astype(BF16).astype(F32)
    lo = (ac_all - hi - mid).astype(BF16).astype(F32)
    for c in range(nc):
        act3_ref[c * q:(c + 1) * q, :] = jnp.concatenate([hi[c], mid[c], lo[c]], axis=0).T.astype(BF16)

    def prep(c, carry):
        rows = chunk_rows(c)
        r0 = c * q

        def conv_silu(u_ref, w_ref, bias_r):
            u = u_ref[rows, :].astype(F32)
            width = u.shape[1]
            before = u_ref[pl.ds(pl.multiple_of(jnp.maximum(r0 - 16, 0), 16), 16), :][15:16, :].astype(F32)
            after = u_ref[pl.ds(pl.multiple_of(jnp.minimum(r0 + q, seq_len - 16), 16), 16), :][0:1, :].astype(F32)
            before = jnp.where(c > 0, before, 0.0)
            after = jnp.where(c < nc - 1, after, 0.0)
            rid = lax.broadcasted_iota(jnp.int32, (q, width), 0)
            prev = jnp.where(rid == 0, before, pltpu.roll(u, 1, axis=0))
            nxt = jnp.where(rid == q - 1, after, pltpu.roll(u, q - 1, axis=0))
            w = w_ref[...] * 0.5
            h = prev * w[0:1] + u * w[1:2] + nxt * w[2:3] + bias_r[...] * 0.5
            return h + h * jnp.tanh(h)

        xc = conv_silu(x_ref, cwx_ref, cbx_ref).astype(BF16)
        xs_ref[rows, :] = xc
        cs_ref[rows, :] = conv_silu(c_ref, cwc_ref, cbc_ref).astype(BF16)
        bt = conv_silu(b_ref, cwb_ref, cbb_ref).T
        bt_ref[c] = bt
        wr = wr_ref[c]

        for j in range(n_pairs):
            lanes = slice(j * 2 * half, (j + 1) * 2 * half)
            xblk = block_diag(xc[:, lanes])
            e1, e2 = 2 * j, 2 * j + 1
            scaled = lambda r: (bt * wr[r:r + 1, :]).astype(BF16)
            lhs = jnp.concatenate([
                jnp.concatenate([scaled(e1), scaled(e2)], axis=1),
                jnp.concatenate([scaled(e_n + e1), scaled(e_n + e2)], axis=1)], axis=0)
            out = jnp.dot(lhs, xblk, preferred_element_type=F32)
            sloc_ref[c, 0, :, lanes] = out[:q]
            sloc_ref[c, 1, :, lanes] = out[q:]
        return carry

    unrolled = nc <= 2
    if unrolled:
        for c in range(nc):
            prep(c, 0)
    else:
        lax.fori_loop(0, nc, prep, 0, unroll=2)

    if has_h0:
        hcur_ref[0] = h0_ref[0].T
        hcur_ref[1] = h0_ref[1].T
    else:
        hcur_ref[...] = jnp.zeros(hcur_ref.shape, F32)

    def carry_states(direction):
        def body(i, carry):
            c = i if direction == 0 else nc - 1 - i
            h = hcur_ref[direction]
            hprev_ref[c, direction] = h.astype(BF16)
            cd = cdb_ref[c]
            r0 = direction * e_n
            decay = jnp.concatenate(
                [jnp.where(first_half[0:1, :], cd[r0 + 2 * j:r0 + 2 * j + 1, :], cd[r0 + 2 * j + 1:r0 + 2 * j + 2, :])
                 for j in range(n_pairs)], axis=1)
            hcur_ref[direction] = h * decay + sloc_ref[c, direction]
            return carry
        if unrolled:
            for i in range(nc):
                body(i, 0)
        else:
            lax.fori_loop(0, nc, body, 0)

    carry_states(0)
    carry_states(1)
    if emit_state:
        st_ref[0] = hcur_ref[0].T
        st_ref[1] = hcur_ref[1].T

    dsum = dsk_ref[0:1, :] + dsk_ref[1:2, :]
    sel_k = lax.broadcasted_iota(jnp.int32, (3 * 2 * e_n, 4 * q), 0) % (2 * e_n)
    sel_n = lax.broadcasted_iota(jnp.int32, (3 * 2 * e_n, 4 * q), 1) // q
    spread_sel = [(sel_k == 2 * j + (sel_n % 2) + e_n * (sel_n // 2)).astype(BF16) for j in range(n_pairs)]

    def emit(c, zero_in_f=False, zero_in_b=False):
        rows = chunk_rows(c)
        cc = cs_ref[rows, :]
        cb = jnp.dot(cc, bt_ref[c].astype(BF16), preferred_element_type=F32)
        off_f = None if zero_in_f else jnp.dot(cc, hprev_ref[c, 0], preferred_element_type=F32)
        off_b = None if zero_in_b else jnp.dot(cc, hprev_ref[c, 1], preferred_element_type=F32)
        act3 = act3_ref[rows, :]
        rp = rowp_ref[c]
        rd = rowd_ref[c]
        spreads = [jnp.dot(act3, spread_sel[j], preferred_element_type=F32) for j in range(n_pairs)]
        for j in range(n_pairs):
            lanes = slice(j * 2 * half, (j + 1) * 2 * half)
            spread = spreads[j]
            acf = [spread[:, 0:q], spread[:, q:2 * q]]
            acb = [spread[:, 2 * q:3 * q], spread[:, 3 * q:4 * q]]
            mix = []
            for i, e in enumerate((2 * j, 2 * j + 1)):
                arg = jnp.where(diag, rd[e:e + 1, :],
                                jnp.where(lower, acf[i] - rp[e:e + 1, :], acb[i] - rp[e_n + e:e_n + e + 1, :]))
                mix.append((cb * jnp.exp2(arg)).astype(BF16))
            xp = xs_ref[rows, lanes]
            y = jnp.dot(jnp.concatenate(mix, axis=1), block_diag(xp), preferred_element_type=F32)
            if off_f is not None:
                y = y + off_f[:, lanes] * jnp.exp2(jnp.where(first_half, acf[0], acf[1]))
            if off_b is not None:
                y = y + off_b[:, lanes] * jnp.exp2(jnp.where(first_half, acb[0], acb[1]))
            y_ref[rows, lanes] = (y + xp.astype(F32) * dsum[:, lanes]).astype(y_ref.dtype)

    if has_h0:
        def emit_body(c, carry):
            emit(c)
            return carry
        lax.fori_loop(0, nc, emit_body, 0, unroll=2)
    else:
        for c in range(nc):
            emit(c, zero_in_f=(c == 0), zero_in_b=(c == nc - 1))


def ssd_scan(proj, dt_hm, conv_w, conv_b, bias_hm, alog_hm, d_skip_wide, *, seq_len, n_seq, row_off,
             seqs_per_step=1, h0=None):
    has_h0 = h0 is not None
    emit_state = not has_h0
    nc = seq_len // SSM_CHUNK
    sps = seqs_per_step
    s_off = row_off // (sps * seq_len)
    gw = GROUP_WIDTH
    ns = SSM_STATE
    e2 = 2 * HEADS_PER_GROUP
    x_blk0 = SSM_D_INNER // gw
    b_blk0 = 2 * SSM_D_INNER // ns
    c_blk0 = b_blk0 + SSM_GROUPS
    cw_b0 = SSM_D_INNER // ns
    in_specs = [
        pl.BlockSpec((sps * seq_len, gw), lambda s, g: (s_off + s, x_blk0 + g)),
        pl.BlockSpec((sps * seq_len, ns), lambda s, g: (s_off + s, b_blk0 + g)),
        pl.BlockSpec((sps * seq_len, ns), lambda s, g: (s_off + s, c_blk0 + g)),
        pl.BlockSpec((None, 3, gw), lambda s, g: (0, 0, g)),
        pl.BlockSpec((None, 3, ns), lambda s, g: (0, 0, cw_b0 + g)),
        pl.BlockSpec((None, 3, ns), lambda s, g: (0, 0, cw_b0 + SSM_GROUPS + g)),
        pl.BlockSpec((None, 1, gw), lambda s, g: (0, 0, g)),
        pl.BlockSpec((None, 1, ns), lambda s, g: (0, 0, cw_b0 + g)),
        pl.BlockSpec((None, 1, ns), lambda s, g: (0, 0, cw_b0 + SSM_GROUPS + g)),
        pl.BlockSpec((None, sps * nc, e2, SSM_CHUNK), lambda s, g: (g, s_off + s, 0, 0)),
        pl.BlockSpec((None, e2, 1), lambda s, g: (g, 0, 0)),
        pl.BlockSpec((None, e2, 1), lambda s, g: (g, 0, 0)),
        pl.BlockSpec((2, gw), lambda s, g: (0, g)),
    ]
    args = [proj, proj, proj, conv_w, conv_w, conv_w, conv_b, conv_b, conv_b,
            dt_hm, bias_hm, alog_hm, d_skip_wide]
    assert len(args) == N_SSD_INPUTS
    state_spec = pl.BlockSpec((sps, 2, gw, ns), lambda s, g: (s, 0, g, 0))
    if has_h0:
        in_specs.append(state_spec)
        args.append(h0)
    y_spec = pl.BlockSpec((sps * seq_len, gw), lambda s, g: (s, g))
    y_shape = jax.ShapeDtypeStruct((n_seq * seq_len, SSM_D_INNER), BF16)
    if emit_state:
        out_specs = [y_spec, state_spec]
        out_shape = [y_shape, jax.ShapeDtypeStruct((n_seq, 2, SSM_D_INNER, ns), F32)]
    else:
        out_specs = y_spec
        out_shape = y_shape
    return pl.pallas_call(
        functools.partial(_ssd_kernel, seq_len=seq_len, has_h0=has_h0, emit_state=emit_state,
                          seqs_per_step=sps),
        grid=(n_seq // sps, SSM_GROUPS),
        in_specs=in_specs,
        out_specs=out_specs,
        out_shape=out_shape,
        scratch_shapes=[
            pltpu.VMEM((sps, seq_len, gw), BF16),
            pltpu.VMEM((sps, nc, ns, SSM_CHUNK), F32),
            pltpu.VMEM((sps, seq_len, ns), BF16),
            pltpu.VMEM((sps, seq_len, 3 * e2), BF16),
            pltpu.VMEM((sps, nc, e2, SSM_CHUNK), F32),
            pltpu.VMEM((sps, nc, HEADS_PER_GROUP, SSM_CHUNK), F32),
            pltpu.VMEM((sps, nc, e2, SSM_CHUNK), F32),
            pltpu.VMEM((sps, nc, e2, SSM_CHUNK), F32),
            pltpu.VMEM((sps, nc, 2, ns, gw), F32),
            pltpu.VMEM((sps, nc, 2, ns, gw), BF16),
            pltpu.VMEM((sps, 2, ns, gw), F32)],
        compiler_params=_params(2),
        name="ssd_scan",
    )(*args)


def _gate_norm_kernel(y_ctx_ref, y_lat_ref, z_ref, g_ref, o_ref, *, split):
    y = jnp.where(pl.program_id(0) < split, y_ctx_ref[...], y_lat_ref[...])
    v = y.astype(F32) * _silu(z_ref[...].astype(F32))
    ms = jnp.mean(v * v, axis=-1, keepdims=True)
    o_ref[...] = (v * lax.rsqrt(ms + NORM_EPS) * g_ref[...]).astype(o_ref.dtype)


def gate_norm(y_ctx, y_lat, proj, gain, tm=512):
    split = N_CTX // tm
    wide = lambda index: pl.BlockSpec((tm, SSM_D_INNER), index)
    return pl.pallas_call(
        functools.partial(_gate_norm_kernel, split=split),
        grid=(N_TOK // tm,),
        in_specs=[wide(lambda m: (jnp.minimum(m, split - 1), 0)),
                  wide(lambda m: (jnp.maximum(m - split, 0), 0)),
                  wide(lambda m: (m, 0)),
                  pl.BlockSpec((1, SSM_D_INNER), lambda m: (0, 0))],
        out_specs=wide(lambda m: (m, 0)),
        out_shape=jax.ShapeDtypeStruct((N_TOK, SSM_D_INNER), BF16),
        compiler_params=_params(1),
        name="gate_norm",
    )(y_ctx, y_lat, proj, gain.reshape(1, SSM_D_INNER))


def _mlp(x, modt, layer, norm_mlp, mlp_w1, mlp_w2):
    h = norm_mod(x, norm_mlp, modt, layer, 3, 4)
    a = matmul(h, mlp_w1, layer=layer, col_off=0, cols=D_FF, tn=1024, tm=2048, out_dtype=BF16, epilogue="relu2")
    half = D_FF // 2
    for k_off in (0, half):
        x = matmul(a, mlp_w2, layer=layer, col_off=0, cols=D_MODEL, tn=512, k_off=k_off, k_len=half,
                   out_dtype=F32, epilogue="resid", x=x, modt=modt, mod_layer=layer, which=5)
    return x


def kernel(x_prompt, x_sample, cache_attn_k, cache_attn_v, state_ssm, c, c_ctx, mod_w, mod_b, norm_mix, norm_mlp, mlp_w1, mlp_w2, na_wqkv, na_wo, na_rpb, ssm_w_in, ssm_conv_w, ssm_conv_b, ssm_dt_bias, ssm_a_log, ssm_d, ssm_norm, ssm_w_out, final_norm):
    x_ctx = x_prompt.reshape(N_CTX, D_MODEL)
    x_lat = x_sample.reshape(N_LAT, D_MODEL)
    cond = jnp.concatenate([c_ctx[None, :], c, jnp.zeros((COND_ROWS - 1 - DEC_BATCH, D_MODEL), F32)], axis=0)
    modt = mod_table(cond, mod_w, mod_b)

    h_ctx = norm_mod(x_ctx, norm_mix, modt, 0, 0, 1)
    h_lat = norm_mod(x_lat, norm_mix, modt, 0, 0, 1, tok_off=N_CTX)
    qkv = functools.partial(matmul, w=na_wqkv, layer=0, cols=D_MODEL, tn=1024)
    q_scale = (NA_HEAD_DIM ** -0.5, D_MODEL // 1024)
    q_ctx = qkv(h_ctx, col_off=0, out_dtype=BF16, scale=q_scale, tm=2048)
    k_ctx = qkv(h_ctx, col_off=D_MODEL, out_dtype=F32)
    v_ctx = qkv(h_ctx, col_off=2 * D_MODEL, out_dtype=F32)
    qkv_lat = matmul(h_lat, na_wqkv, layer=0, col_off=0, cols=3 * D_MODEL, tn=1024, tm=2048, out_dtype=BF16,
                     scale=q_scale)
    o_ctx = ctx_attention(q_ctx, k_ctx, v_ctx)
    n_attn = cache_attn_k.shape[1]
    o_lat = neighborhood_attention(
        qkv_lat,
        cache_attn_k.reshape(DEC_BATCH, n_attn * PAST_LEN, D_MODEL),
        cache_attn_v.reshape(DEC_BATCH, n_attn * PAST_LEN, D_MODEL),
        0, _na_bias_table(na_rpb[0]))
    x = matmul((o_ctx, o_lat), na_wo, layer=0, col_off=0, cols=D_MODEL, tn=1024, tm=512, out_dtype=F32,
               epilogue="resid", x=(x_ctx, x_lat), modt=modt, mod_layer=0, which=2)
    x = _mlp(x, modt, 0, norm_mlp, mlp_w1, mlp_w2)

    h = norm_mod(x, norm_mix, modt, 1, 0, 1)
    zxbc_cols = SSM_D_INNER + SSM_CONV_DIM
    proj = matmul(h, ssm_w_in, layer=0, col_off=0, cols=zxbc_cols, tn=1024, tm=2048, out_dtype=BF16)
    dt_raw = matmul(h, ssm_w_in, layer=0, col_off=zxbc_cols, cols=2 * SSM_HEADS, tn=128, out_dtype=F32)
    e_n = HEADS_PER_GROUP
    dt_hm = dt_raw.reshape(N_TOK // SSM_CHUNK, SSM_CHUNK, 2, SSM_GROUPS, e_n).transpose(3, 0, 2, 4, 1)
    dt_hm = dt_hm.reshape(SSM_GROUPS, N_TOK // SSM_CHUNK, 2 * e_n, SSM_CHUNK)
    by_group = lambda p: p.reshape(2, SSM_GROUPS, e_n).transpose(1, 0, 2).reshape(SSM_GROUPS, 2 * e_n, 1)
    n_ssm = state_ssm.shape[1]
    ssd = functools.partial(
        ssd_scan, proj, dt_hm, ssm_conv_w, ssm_conv_b.reshape(1, 1, SSM_CONV_DIM),
        by_group(ssm_dt_bias[0]), by_group(ssm_a_log[0]), jnp.repeat(ssm_d[0], SSM_HEAD_DIM, axis=1))
    y_ctx, new_state = ssd(seq_len=SEQ, n_seq=BATCH, row_off=0, seqs_per_step=8)
    y_lat = ssd(seq_len=DEC_SEQ, n_seq=DEC_BATCH, row_off=N_CTX,
                h0=state_ssm.reshape(DEC_BATCH, n_ssm * 2, SSM_D_INNER, SSM_STATE))
    yn = gate_norm(y_ctx, y_lat, proj, ssm_norm)
    x = matmul(yn, ssm_w_out, layer=0, col_off=0, cols=D_MODEL, tn=512, out_dtype=F32,
               epilogue="resid", x=x, modt=modt, mod_layer=1, which=2)
    x = _mlp(x, modt, 1, norm_mlp, mlp_w1, mlp_w2)

    y_prompt = final_rmsnorm(x, final_norm, 0, N_CTX).reshape(BATCH, SEQ, D_MODEL)
    y_sample = final_rmsnorm(x, final_norm, N_CTX, N_LAT).reshape(DEC_BATCH, DEC_SEQ, D_MODEL)
    kv_shape = (BATCH, 1, SEQ, NA_HEADS, NA_HEAD_DIM)
    return (y_prompt, y_sample, k_ctx.reshape(kv_shape), v_ctx.reshape(kv_shape),
            new_state.reshape(BATCH, 1, 2, SSM_HEADS, SSM_HEAD_DIM, SSM_STATE))
```

```python
import functools

import jax
import jax.numpy as jnp
import numpy as np
from jax import lax
from jax.experimental import pallas as pl
from jax.experimental.pallas import tpu as pltpu

F32 = jnp.float32
BF16 = jnp.bfloat16

D_MODEL = 2048
BATCH = 32
SEQ = 256
DEPTH = 2
DEC_BATCH = 4
DEC_SEQ = 1024
PAST_LEN = 256
GRID_W = 64
NA_HEADS = 16
NA_HEAD_DIM = 128
WIN_ROWS = 8
WIN_COLS = 16
SSM_D_INNER = 4096
SSM_HEAD_DIM = 64
SSM_HEADS = 64
SSM_GROUPS = 8
SSM_STATE = 128
SSM_CHUNK = 128
SSM_CONV_DIM = SSM_D_INNER + 2 * SSM_GROUPS * SSM_STATE
D_FF = 4 * D_MODEL
NORM_EPS = 1e-6
NEG_INF = -1e30
LOG2E = 1.4426950408889634

N_CTX = BATCH * SEQ
N_LAT = DEC_BATCH * DEC_SEQ
N_TOK = N_CTX + N_LAT
COND_ROWS = 8
HEADS_PER_GROUP = SSM_HEADS // SSM_GROUPS
GROUP_WIDTH = HEADS_PER_GROUP * SSM_HEAD_DIM

VMEM_LIMIT_BYTES = 56 * 1024 * 1024


def _params(n_axes):
    return pltpu.CompilerParams(dimension_semantics=("arbitrary",) * n_axes,
                                vmem_limit_bytes=VMEM_LIMIT_BYTES)


def _silu(v):
    h = 0.5 * v
    return h + h * jnp.tanh(h)


def _softplus(x):
    return jnp.maximum(x, 0.0) + jnp.log(1.0 + jnp.exp(-jnp.abs(x)))


def _cond_row(row_start):
    return jnp.where(row_start < N_CTX, 0, 1 + (row_start - N_CTX) // DEC_SEQ)


def _mod_kernel(cond_ref, w_ref, b_ref, o_ref):
    s = _silu(cond_ref[...]).astype(BF16)
    part = jnp.dot(s, w_ref[...].astype(BF16), preferred_element_type=F32)

    @pl.when(pl.program_id(1) == 0)
    def _():
        o_ref[...] = part + b_ref[...]

    @pl.when(pl.program_id(1) > 0)
    def _():
        o_ref[...] += part


def mod_table(cond, mod_w, mod_b):
    tk = 256
    n6 = 6 * D_MODEL
    out = pl.pallas_call(
        _mod_kernel,
        grid=(DEPTH, D_MODEL // tk),
        in_specs=[pl.BlockSpec((COND_ROWS, tk), lambda l, k: (0, k)),
                  pl.BlockSpec((None, tk, n6), lambda l, k: (l, k, 0)),
                  pl.BlockSpec((None, 1, n6), lambda l, k: (l, 0, 0))],
        out_specs=pl.BlockSpec((None, COND_ROWS, n6), lambda l, k: (l, 0, 0)),
        out_shape=jax.ShapeDtypeStruct((DEPTH, COND_ROWS, n6), F32),
        compiler_params=_params(2),
        name="mod_table",
    )(cond, mod_w, mod_b.reshape(DEPTH, 1, n6))
    return out.reshape(DEPTH, COND_ROWS, 6, 1, D_MODEL)


def _norm_mod_kernel(x_ref, g_ref, shift_ref, scale_ref, o_ref):
    x = x_ref[...]
    ms = jnp.mean(x * x, axis=-1, keepdims=True)
    y = x * lax.rsqrt(ms + NORM_EPS) * g_ref[...]
    o_ref[...] = (y * (1.0 + scale_ref[...]) + shift_ref[...]).astype(o_ref.dtype)


def norm_mod(x, gain, modt, layer, which_shift, which_scale, tok_off=0, tm=1024):
    n = x.shape[0]
    mod_spec = lambda which: pl.BlockSpec(
        (None, None, None, 1, D_MODEL), lambda m: (layer, _cond_row(tok_off + m * tm), which, 0, 0))
    return pl.pallas_call(
        _norm_mod_kernel,
        grid=(n // tm,),
        in_specs=[pl.BlockSpec((tm, D_MODEL), lambda m: (m, 0)),
                  pl.BlockSpec((None, 1, D_MODEL), lambda m: (layer, 0, 0)),
                  mod_spec(which_shift), mod_spec(which_scale)],
        out_specs=pl.BlockSpec((tm, D_MODEL), lambda m: (m, 0)),
        out_shape=jax.ShapeDtypeStruct((n, D_MODEL), BF16),
        compiler_params=_params(1),
        name="norm_mod",
    )(x, gain.reshape(-1, 1, D_MODEL), modt, modt)


def _final_norm_kernel(x_ref, g_ref, o_ref):
    x = x_ref[...]
    ms = jnp.mean(x * x, axis=-1, keepdims=True)
    o_ref[...] = x * lax.rsqrt(ms + NORM_EPS) * g_ref[...]


def final_rmsnorm(x, gain, row_off, rows, tm=1024):
    off = row_off // tm
    return pl.pallas_call(
        _final_norm_kernel,
        grid=(rows // tm,),
        in_specs=[pl.BlockSpec((tm, D_MODEL), lambda m: (off + m, 0)),
                  pl.BlockSpec((1, D_MODEL), lambda m: (0, 0))],
        out_specs=pl.BlockSpec((tm, D_MODEL), lambda m: (m, 0)),
        out_shape=jax.ShapeDtypeStruct((rows, D_MODEL), F32),
        compiler_params=_params(1),
        name="final_norm",
    )(x, gain.reshape(1, D_MODEL))


def _mm_kernel(*refs, epilogue, scale, n_src, split):
    refs = list(refs)
    a_refs = [refs.pop(0) for _ in range(n_src)]
    w_ref = refs.pop(0)
    resid = epilogue == "resid"
    x_refs = [refs.pop(0) for _ in range(n_src)] if resid else [None] * n_src
    gate_ref = refs.pop(0) if resid else None
    o_ref, wb_ref = refs
    m = pl.program_id(1)

    @pl.when(m == 0)
    def _():
        wb_ref[...] = w_ref[...].astype(BF16)

    def step(a_ref, x_ref):
        acc = jnp.dot(a_ref[...], wb_ref[...], preferred_element_type=F32)
        if epilogue == "cast":
            if scale is not None:
                acc = jnp.where(pl.program_id(0) < scale[1], acc * scale[0], acc)
            o_ref[...] = acc.astype(o_ref.dtype)
        elif epilogue == "relu2":
            r = jnp.maximum(acc, 0.0)
            o_ref[...] = (r * r).astype(o_ref.dtype)
        else:
            o_ref[...] = x_ref[...] + gate_ref[...] * acc

    if n_src == 1:
        step(a_refs[0], x_refs[0])
    else:
        pl.when(m < split)(lambda: step(a_refs[0], x_refs[0]))
        pl.when(m >= split)(lambda: step(a_refs[1], x_refs[1]))


def matmul(a, w, *, layer, col_off, cols, tn, out_dtype, tm=1024, k_off=0, k_len=None, epilogue="cast",
           scale=None, x=None, modt=None, mod_layer=None, which=None):
    a_parts = a if isinstance(a, tuple) else (a,)
    x_parts = () if x is None else (x if isinstance(x, tuple) else (x,))
    tk = a_parts[0].shape[1] if k_len is None else k_len
    rows = sum(p.shape[0] for p in a_parts)
    n_off, kb_off = col_off // tn, k_off // tk
    split = N_CTX // tm

    def row_specs(parts, width, col_block):
        if len(parts) == 1:
            return [pl.BlockSpec((tm, width), lambda n, m: (m, col_block(n)))]
        return [pl.BlockSpec((tm, width), lambda n, m: (jnp.minimum(m, split - 1), col_block(n))),
                pl.BlockSpec((tm, width), lambda n, m: (jnp.maximum(m - split, 0), col_block(n)))]

    in_specs = row_specs(a_parts, tk, lambda n: kb_off)
    in_specs.append(pl.BlockSpec((None, tk, tn), lambda n, m: (layer, kb_off, n_off + n)))
    args = list(a_parts) + [w]
    if epilogue == "resid":
        assert len(x_parts) == len(a_parts)
        in_specs += row_specs(x_parts, tn, lambda n: n)
        in_specs.append(pl.BlockSpec((None, None, None, 1, tn),
                                     lambda n, m: (mod_layer, _cond_row(m * tm), which, 0, n)))
        args += list(x_parts) + [modt]
    return pl.pallas_call(
        functools.partial(_mm_kernel, epilogue=epilogue, scale=scale, n_src=len(a_parts), split=split),
        grid=(cols // tn, rows // tm),
        in_specs=in_specs,
        out_specs=pl.BlockSpec((tm, tn), lambda n, m: (m, n)),
        out_shape=jax.ShapeDtypeStruct((rows, cols), out_dtype),
        scratch_shapes=[pltpu.VMEM((tk, tn), BF16)],
        compiler_params=_params(2),
        name="matmul_" + epilogue,
    )(*args)


def _norm_mm_kernel(x_ref, g_ref, shift_ref, scale_ref, w_ref, o_ref, h_ref, *, epilogue):
    @pl.when(pl.program_id(1) == 0)
    def _():
        x = x_ref[...]
        ms = jnp.mean(x * x, axis=-1, keepdims=True)
        y = x * lax.rsqrt(ms + NORM_EPS) * g_ref[...]
        h_ref[...] = (y * (1.0 + scale_ref[...]) + shift_ref[...]).astype(h_ref.dtype)

    acc = jnp.dot(h_ref[...], w_ref[...].astype(BF16), preferred_element_type=F32)
    if epilogue == "relu2":
        r = jnp.maximum(acc, 0.0)
        acc = r * r
    o_ref[...] = acc.astype(o_ref.dtype)


def norm_matmul(x, gain, modt, w, *, layer, w_layer, which_shift, which_scale, cols, tn, epilogue="cast", tm=1024):
    rows = x.shape[0]
    mod_spec = lambda which: pl.BlockSpec(
        (None, None, None, 1, D_MODEL), lambda m, n: (layer, _cond_row(m * tm), which, 0, 0))
    row_tile = pl.BlockSpec((tm, D_MODEL), lambda m, n: (m, 0))
    return pl.pallas_call(
        functools.partial(_norm_mm_kernel, epilogue=epilogue),
        grid=(rows // tm, cols // tn),
        in_specs=[row_tile,
                  pl.BlockSpec((None, 1, D_MODEL), lambda m, n: (layer, 0, 0)),
                  mod_spec(which_shift), mod_spec(which_scale),
                  pl.BlockSpec((None, D_MODEL, tn), lambda m, n: (w_layer, 0, n))],
        out_specs=[pl.BlockSpec((tm, tn), lambda m, n: (m, n)), row_tile],
        out_shape=[jax.ShapeDtypeStruct((rows, cols), BF16), jax.ShapeDtypeStruct((rows, D_MODEL), BF16)],
        compiler_params=_params(2),
        name="norm_matmul_" + epilogue,
    )(x, gain.reshape(-1, 1, D_MODEL), modt, modt, w)


def _ctx_attn_kernel(q_ref, k_ref, v_ref, o_ref):
    for h in range(NA_HEADS):
        sl = slice(h * NA_HEAD_DIM, (h + 1) * NA_HEAD_DIM)
        q = q_ref[:, sl]
        k = k_ref[:, sl].astype(BF16)
        v = v_ref[:, sl].astype(BF16)
        s = lax.dot_general(q, k, (((1,), (1,)), ((), ())), preferred_element_type=F32)
        p = jnp.exp(s - jnp.max(s, axis=-1, keepdims=True))
        l = jnp.sum(p, axis=-1, keepdims=True)
        o = jnp.dot(p.astype(BF16), v, preferred_element_type=F32) / l
        o_ref[:, sl] = o.astype(o_ref.dtype)


def ctx_attention(q, k, v):
    spec = pl.BlockSpec((SEQ, D_MODEL), lambda b: (b, 0))
    return pl.pallas_call(
        _ctx_attn_kernel,
        grid=(BATCH,),
        in_specs=[spec, spec, spec],
        out_specs=spec,
        out_shape=jax.ShapeDtypeStruct((N_CTX, D_MODEL), BF16),
        compiler_params=_params(1),
        name="ctx_attention",
    )(q, k, v)


GRID_ROWS = DEC_SEQ // GRID_W
Q_BLOCK_ROWS = 4
KEY_WIN_ROWS = 12
N_ROW_OFFS = 2 * WIN_ROWS - 1
TILE_BOTH, TILE_SECOND, TILE_FIRST = 0, N_ROW_OFFS - 1, 2 * N_ROW_OFFS - 1
TILE_NONE = 3 * N_ROW_OFFS - 1
N_BIAS_TILES = TILE_NONE + 1


def _row_start(r):
    return min(max(r - WIN_ROWS // 2, 0), GRID_ROWS - WIN_ROWS)


def _key_win_start(rb):
    return min(max(rb * Q_BLOCK_ROWS - WIN_ROWS // 2, 0), GRID_ROWS - KEY_WIN_ROWS)


def _na_bias_table(rpb):
    reach = WIN_COLS - 1
    qc = np.arange(GRID_W)[:, None]
    kc = np.arange(GRID_W)[None, :]
    ws = np.clip(qc - WIN_COLS // 2, 0, GRID_W - WIN_COLS)
    in_window = (kc >= ws) & (kc < ws + WIN_COLS)
    col_off = np.clip(kc - qc, -reach, reach) + reach
    pick = ((np.arange(2 * reach + 1)[:, None, None] == col_off[None]) & in_window[None]).astype(np.float32)
    t = jnp.einsum("hdk,kqc->hdqc", rpb, pick, precision=lax.Precision.HIGHEST)
    t = jnp.where(in_window, t, NEG_INF)
    masked = jnp.full_like(t, NEG_INF)
    return jnp.concatenate([
        jnp.concatenate([t[:, :-1], t[:, 1:]], axis=-1),
        jnp.concatenate([masked, t], axis=-1),
        jnp.concatenate([t, masked], axis=-1),
        jnp.concatenate([masked[:, :1], masked[:, :1]], axis=-1)], axis=1)


def _bias_tile_index(r, key_row):
    rs = _row_start(r)
    first_ok = rs <= key_row < rs + WIN_ROWS
    second_ok = rs <= key_row + 1 < rs + WIN_ROWS
    off = key_row - r + WIN_ROWS - 1
    if first_ok and second_ok:
        return TILE_BOTH + off
    if second_ok:
        return TILE_SECOND + off + 1
    if first_ok:
        return TILE_FIRST + off
    return TILE_NONE


def _na_kernel(q_ref, k_ref, v_ref, ck_ref, cv_ref, t_ref, o_ref):
    ck = ck_ref[...].astype(BF16)
    cv = cv_ref[...].astype(BF16)
    nt = (((1,), (1,)), ((), ()))
    blk = Q_BLOCK_ROWS * GRID_W
    for rb in range(GRID_ROWS // Q_BLOCK_ROWS):
        kw0 = _key_win_start(rb)
        keys = slice(kw0 * GRID_W, (kw0 + KEY_WIN_ROWS) * GRID_W)
        q = q_ref[rb * blk:(rb + 1) * blk, :]
        bias = jnp.concatenate([
            jnp.concatenate([t_ref[_bias_tile_index(rb * Q_BLOCK_ROWS + i, kw0 + 2 * p)]
                             for p in range(KEY_WIN_ROWS // 2)], axis=1)
            for i in range(Q_BLOCK_ROWS)], axis=0)
        s_loc = lax.dot_general(q, k_ref[keys, :], nt, preferred_element_type=F32) + bias
        s_ctx = lax.dot_general(q, ck, nt, preferred_element_type=F32)
        mx = jnp.maximum(jnp.max(s_loc, axis=-1, keepdims=True), jnp.max(s_ctx, axis=-1, keepdims=True))
        p_loc = jnp.exp(s_loc - mx)
        p_ctx = jnp.exp(s_ctx - mx)
        l = jnp.sum(p_loc, axis=-1, keepdims=True) + jnp.sum(p_ctx, axis=-1, keepdims=True)
        o = (jnp.dot(p_loc.astype(BF16), v_ref[keys, :], preferred_element_type=F32)
             + jnp.dot(p_ctx.astype(BF16), cv, preferred_element_type=F32)) / l
        o_ref[rb * blk:(rb + 1) * blk, :] = o.astype(o_ref.dtype)


def neighborhood_attention(qkv, cache_k, cache_v, cache_layer, bias_table):
    tok = lambda part: pl.BlockSpec((DEC_SEQ, NA_HEAD_DIM), lambda h, b: (b, part * NA_HEADS + h))
    cache = pl.BlockSpec((None, PAST_LEN, NA_HEAD_DIM), lambda h, b: (b, cache_layer, h))
    return pl.pallas_call(
        _na_kernel,
        grid=(NA_HEADS, DEC_BATCH),
        in_specs=[tok(0), tok(1), tok(2), cache, cache,
                  pl.BlockSpec((None, N_BIAS_TILES, GRID_W, 2 * GRID_W), lambda h, b: (h, 0, 0, 0))],
        out_specs=tok(0),
        out_shape=jax.ShapeDtypeStruct((N_LAT, D_MODEL), BF16),
        compiler_params=_params(2),
        name="neighborhood_attention",
    )(qkv, qkv, qkv, cache_k, cache_v, bias_table)


N_SSD_INPUTS = 13


def _ssd_kernel(*refs, seq_len, has_h0, emit_state, seqs_per_step):
    n_in = N_SSD_INPUTS + (1 if has_h0 else 0)
    n_out = 1 + (1 if emit_state else 0)
    ins, outs, scratch = refs[:n_in], refs[n_in:n_in + n_out], refs[n_in + n_out:]
    nc = seq_len // SSM_CHUNK
    for s in range(seqs_per_step):
        tok = pl.ds(s * seq_len, seq_len)
        views = [ins[0].at[tok], ins[1].at[tok], ins[2].at[tok]] + list(ins[3:9])
        views += [ins[9].at[pl.ds(s * nc, nc)]] + list(ins[10:N_SSD_INPUTS])
        if has_h0:
            views.append(ins[N_SSD_INPUTS].at[s])
        views.append(outs[0].at[tok])
        if emit_state:
            views.append(outs[1].at[s])
        views += [r.at[s] for r in scratch]
        _ssd_sequence(*views, seq_len=seq_len, has_h0=has_h0, emit_state=emit_state)


def _ssd_sequence(*refs, seq_len, has_h0, emit_state):
    (x_ref, b_ref, c_ref, cwx_ref, cwb_ref, cwc_ref, cbx_ref, cbb_ref, cbc_ref,
     dt_ref, bias_ref, alog_ref, dsk_ref) = refs[:N_SSD_INPUTS]
    rest = list(refs[N_SSD_INPUTS:])
    h0_ref = rest.pop(0) if has_h0 else None
    y_ref = rest.pop(0)
    st_ref = rest.pop(0) if emit_state else None
    (xs_ref, bt_ref, cs_ref, act3_ref, rowp_ref, rowd_ref, cdb_ref, wr_ref, sloc_ref, hprev_ref, hcur_ref) = rest

    q = SSM_CHUNK
    nc = seq_len // q
    hp = lax.Precision.HIGHEST
    e_n = HEADS_PER_GROUP
    half = SSM_HEAD_DIM
    n_pairs = e_n // 2

    li = lax.broadcasted_iota(jnp.int32, (q, q), 0)
    si = lax.broadcasted_iota(jnp.int32, (q, q), 1)
    tril = (si <= li).astype(F32)
    triu = (si >= li).astype(F32)
    lower = si <= li
    diag = si == li
    first_half = si < half
    fwd_row = lax.broadcasted_iota(jnp.int32, (2 * e_n, q), 0) < e_n
    a_hm = -jnp.exp(alog_ref[...]) * LOG2E

    def chunk_rows(c):
        return pl.ds(c * q, q) if isinstance(c, int) else pl.ds(pl.multiple_of(c * q, q), q)

    def block_diag(pair):
        zero = jnp.zeros_like(pair)
        return jnp.concatenate([jnp.where(first_half, pair, zero), jnp.where(first_half, zero, pair)], axis=0)

    dt_all = _softplus(dt_ref[...] + bias_ref[...])
    da_all = (dt_all * a_hm).reshape(nc * 2 * e_n, q)
    fwd_rows = (lax.broadcasted_iota(jnp.int32, (nc * 2 * e_n, q), 0) % (2 * e_n)) < e_n
    ac_all = jnp.where(fwd_rows,
                       jnp.dot(da_all, triu, precision=hp, preferred_element_type=F32),
                       jnp.dot(da_all, tril, precision=hp, preferred_element_type=F32)).reshape(nc, 2 * e_n, q)
    rowp_ref[...] = ac_all - jnp.log2(dt_all)
    rowd_ref[...] = jnp.log2(dt_all[:, :e_n, :] + dt_all[:, e_n:, :])
    ends = jnp.broadcast_to(jnp.where(fwd_row[None, :, 0:1], ac_all[:, :, q - 1:q], ac_all[:, :, 0:1]), ac_all.shape)
    wr_ref[...] = jnp.exp2(ends - ac_all) * dt_all
    cdb_ref[...] = jnp.exp2(ends)
    hi = ac_all.astype(BF16).astype(F32)
    mid = (ac_all - hi).astype(BF16).astype(F32)
    lo = (ac_all - hi - mid).astype(BF16).astype(F32)
    for c in range(nc):
        act3_ref[c * q:(c + 1) * q, :] = jnp.concatenate([hi[c], mid[c], lo[c]], axis=0).T.astype(BF16)

    def prep(c, carry):
        rows = chunk_rows(c)
        r0 = c * q

        def conv_silu(u_ref, w_ref, bias_r):
            u = u_ref[rows, :].astype(F32)
            width = u.shape[1]
            before = u_ref[pl.ds(pl.multiple_of(jnp.maximum(r0 - 16, 0), 16), 16), :][15:16, :].astype(F32)
            after = u_ref[pl.ds(pl.multiple_of(jnp.minimum(r0 + q, seq_len - 16), 16), 16), :][0:1, :].astype(F32)
            before = jnp.where(c > 0, before, 0.0)
            after = jnp.where(c < nc - 1, after, 0.0)
            rid = lax.broadcasted_iota(jnp.int32, (q, width), 0)
            prev = jnp.where(rid == 0, before, pltpu.roll(u, 1, axis=0))
            nxt = jnp.where(rid == q - 1, after, pltpu.roll(u, q - 1, axis=0))
            w = w_ref[...] * 0.5
            h = prev * w[0:1] + u * w[1:2] + nxt * w[2:3] + bias_r[...] * 0.5
            return h + h * jnp.tanh(h)

        xc = conv_silu(x_ref, cwx_ref, cbx_ref).astype(BF16)
        xs_ref[rows, :] = xc
        cs_ref[rows, :] = conv_silu(c_ref, cwc_ref, cbc_ref).astype(BF16)
        bt = conv_silu(b_ref, cwb_ref, cbb_ref).T
        bt_ref[c] = bt
        wr = wr_ref[c]

        for j in range(n_pairs):
            lanes = slice(j * 2 * half, (j + 1) * 2 * half)
            xblk = block_diag(xc[:, lanes])
            e1, e2 = 2 * j, 2 * j + 1
            scaled = lambda r: (bt * wr[r:r + 1, :]).astype(BF16)
            lhs = jnp.concatenate([
                jnp.concatenate([scaled(e1), scaled(e2)], axis=1),
                jnp.concatenate([scaled(e_n + e1), scaled(e_n + e2)], axis=1)], axis=0)
            out = jnp.dot(lhs, xblk, preferred_element_type=F32)
            sloc_ref[c, 0, :, lanes] = out[:q]
            sloc_ref[c, 1, :, lanes] = out[q:]
        return carry

    unrolled = nc <= 2
    if unrolled:
        for c in range(nc):
            prep(c, 0)
    else:
        lax.fori_loop(0, nc, prep, 0, unroll=2)

    if has_h0:
        hcur_ref[0] = h0_ref[0].T
        hcur_ref[1] = h0_ref[1].T
    else:
        hcur_ref[...] = jnp.zeros(hcur_ref.shape, F32)

    def carry_states(direction):
        def body(i, carry):
            c = i if direction == 0 else nc - 1 - i
            h = hcur_ref[direction]
            hprev_ref[c, direction] = h.astype(BF16)
            cd = cdb_ref[c]
            r0 = direction * e_n
            decay = jnp.concatenate(
                [jnp.where(first_half[0:1, :], cd[r0 + 2 * j:r0 + 2 * j + 1, :], cd[r0 + 2 * j + 1:r0 + 2 * j + 2, :])
                 for j in range(n_pairs)], axis=1)
            hcur_ref[direction] = h * decay + sloc_ref[c, direction]
            return carry
        if unrolled:
            for i in range(nc):
                body(i, 0)
        else:
            lax.fori_loop(0, nc, body, 0)

    carry_states(0)
    carry_states(1)
    if emit_state:
        st_ref[0] = hcur_ref[0].T
        st_ref[1] = hcur_ref[1].T

    dsum = dsk_ref[0:1, :] + dsk_ref[1:2, :]
    sel_k = lax.broadcasted_iota(jnp.int32, (3 * 2 * e_n, 4 * q), 0) % (2 * e_n)
    sel_n = lax.broadcasted_iota(jnp.int32, (3 * 2 * e_n, 4 * q), 1) // q
    spread_sel = [(sel_k == 2 * j + (sel_n % 2) + e_n * (sel_n // 2)).astype(BF16) for j in range(n_pairs)]

    def emit(c, zero_in_f=False, zero_in_b=False):
        rows = chunk_rows(c)
        cc = cs_ref[rows, :]
        cb = jnp.dot(cc, bt_ref[c].astype(BF16), preferred_element_type=F32)
        off_f = None if zero_in_f else jnp.dot(cc, hprev_ref[c, 0], preferred_element_type=F32)
        off_b = None if zero_in_b else jnp.dot(cc, hprev_ref[c, 1], preferred_element_type=F32)
        act3 = act3_ref[rows, :]
        rp = rowp_ref[c]
        rd = rowd_ref[c]
        spreads = [jnp.dot(act3, spread_sel[j], preferred_element_type=F32) for j in range(n_pairs)]
        for j in range(n_pairs):
            lanes = slice(j * 2 * half, (j + 1) * 2 * half)
            spread = spreads[j]
            acf = [spread[:, 0:q], spread[:, q:2 * q]]
            acb = [spread[:, 2 * q:3 * q], spread[:, 3 * q:4 * q]]
            mix = []
            for i, e in enumerate((2 * j, 2 * j + 1)):
                arg = jnp.where(diag, rd[e:e + 1, :],
                                jnp.where(lower, acf[i] - rp[e:e + 1, :], acb[i] - rp[e_n + e:e_n + e + 1, :]))
                mix.append((cb * jnp.exp2(arg)).astype(BF16))
            xp = xs_ref[rows, lanes]
            y = jnp.dot(jnp.concatenate(mix, axis=1), block_diag(xp), preferred_element_type=F32)
            if off_f is not None:
                y = y + off_f[:, lanes] * jnp.exp2(jnp.where(first_half, acf[0], acf[1]))
            if off_b is not None:
                y = y + off_b[:, lanes] * jnp.exp2(jnp.where(first_half, acb[0], acb[1]))
            y_ref[rows, lanes] = (y + xp.astype(F32) * dsum[:, lanes]).astype(y_ref.dtype)

    if has_h0:
        def emit_body(c, carry):
            emit(c)
            return carry
        lax.fori_loop(0, nc, emit_body, 0, unroll=2)
    else:
        for c in range(nc):
            emit(c, zero_in_f=(c == 0), zero_in_b=(c == nc - 1))


def ssd_scan(proj, dt_hm, conv_w, conv_b, bias_hm, alog_hm, d_skip_wide, *, seq_len, n_seq, row_off,
             seqs_per_step=1, h0=None):
    has_h0 = h0 is not None
    emit_state = not has_h0
    nc = seq_len // SSM_CHUNK
    sps = seqs_per_step
    s_off = row_off // (sps * seq_len)
    gw = GROUP_WIDTH
    ns = SSM_STATE
    e2 = 2 * HEADS_PER_GROUP
    x_blk0 = SSM_D_INNER // gw
    b_blk0 = 2 * SSM_D_INNER // ns
    c_blk0 = b_blk0 + SSM_GROUPS
    cw_b0 = SSM_D_INNER // ns
    in_specs = [
        pl.BlockSpec((sps * seq_len, gw), lambda s, g: (s_off + s, x_blk0 + g)),
        pl.BlockSpec((sps * seq_len, ns), lambda s, g: (s_off + s, b_blk0 + g)),
        pl.BlockSpec((sps * seq_len, ns), lambda s, g: (s_off + s, c_blk0 + g)),
        pl.BlockSpec((None, 3, gw), lambda s, g: (0, 0, g)),
        pl.BlockSpec((None, 3, ns), lambda s, g: (0, 0, cw_b0 + g)),
        pl.BlockSpec((None, 3, ns), lambda s, g: (0, 0, cw_b0 + SSM_GROUPS + g)),
        pl.BlockSpec((None, 1, gw), lambda s, g: (0, 0, g)),
        pl.BlockSpec((None, 1, ns), lambda s, g: (0, 0, cw_b0 + g)),
        pl.BlockSpec((None, 1, ns), lambda s, g: (0, 0, cw_b0 + SSM_GROUPS + g)),
        pl.BlockSpec((None, sps * nc, e2, SSM_CHUNK), lambda s, g: (g, s_off + s, 0, 0)),
        pl.BlockSpec((None, e2, 1), lambda s, g: (g, 0, 0)),
        pl.BlockSpec((None, e2, 1), lambda s, g: (g, 0, 0)),
        pl.BlockSpec((2, gw), lambda s, g: (0, g)),
    ]
    args = [proj, proj, proj, conv_w, conv_w, conv_w, conv_b, conv_b, conv_b,
            dt_hm, bias_hm, alog_hm, d_skip_wide]
    assert len(args) == N_SSD_INPUTS
    state_spec = pl.BlockSpec((sps, 2, gw, ns), lambda s, g: (s, 0, g, 0))
    if has_h0:
        in_specs.append(state_spec)
        args.append(h0)
    y_spec = pl.BlockSpec((sps * seq_len, gw), lambda s, g: (s, g))
    y_shape = jax.ShapeDtypeStruct((n_seq * seq_len, SSM_D_INNER), BF16)
    if emit_state:
        out_specs = [y_spec, state_spec]
        out_shape = [y_shape, jax.ShapeDtypeStruct((n_seq, 2, SSM_D_INNER, ns), F32)]
    else:
        out_specs = y_spec
        out_shape = y_shape
    return pl.pallas_call(
        functools.partial(_ssd_kernel, seq_len=seq_len, has_h0=has_h0, emit_state=emit_state,
                          seqs_per_step=sps),
        grid=(n_seq // sps, SSM_GROUPS),
        in_specs=in_specs,
        out_specs=out_specs,
        out_shape=out_shape,
        scratch_shapes=[
            pltpu.VMEM((sps, seq_len, gw), BF16),
            pltpu.VMEM((sps, nc, ns, SSM_CHUNK), F32),
            pltpu.VMEM((sps, seq_len, ns), BF16),
            pltpu.VMEM((sps, seq_len, 3 * e2), BF16),
            pltpu.VMEM((sps, nc, e2, SSM_CHUNK), F32),
            pltpu.VMEM((sps, nc, HEADS_PER_GROUP, SSM_CHUNK), F32),
            pltpu.VMEM((sps, nc, e2, SSM_CHUNK), F32),
            pltpu.VMEM((sps, nc, e2, SSM_CHUNK), F32),
            pltpu.VMEM((sps, nc, 2, ns, gw), F32),
            pltpu.VMEM((sps, nc, 2, ns, gw), BF16),
            pltpu.VMEM((sps, 2, ns, gw), F32)],
        compiler_params=_params(2),
        name="ssd_scan",
    )(*args)


def _gate_norm_kernel(y_ctx_ref, y_lat_ref, z_ref, g_ref, o_ref, *, split):
    y = jnp.where(pl.program_id(0) < split, y_ctx_ref[...], y_lat_ref[...])
    v = y.astype(F32) * _silu(z_ref[...].astype(F32))
    ms = jnp.mean(v * v, axis=-1, keepdims=True)
    o_ref[...] = (v * lax.rsqrt(ms + NORM_EPS) * g_ref[...]).astype(o_ref.dtype)


def gate_norm(y_ctx, y_lat, proj, gain, tm=512):
    split = N_CTX // tm
    wide = lambda index: pl.BlockSpec((tm, SSM_D_INNER), index)
    return pl.pallas_call(
        functools.partial(_gate_norm_kernel, split=split),
        grid=(N_TOK // tm,),
        in_specs=[wide(lambda m: (jnp.minimum(m, split - 1), 0)),
                  wide(lambda m: (jnp.maximum(m - split, 0), 0)),
                  wide(lambda m: (m, 0)),
                  pl.BlockSpec((1, SSM_D_INNER), lambda m: (0, 0))],
        out_specs=wide(lambda m: (m, 0)),
        out_shape=jax.ShapeDtypeStruct((N_TOK, SSM_D_INNER), BF16),
        compiler_params=_params(1),
        name="gate_norm",
    )(y_ctx, y_lat, proj, gain.reshape(1, SSM_D_INNER))


def _mlp(x, modt, layer, norm_mlp, mlp_w1, mlp_w2):
    a, _ = norm_matmul(x, norm_mlp, modt, mlp_w1, layer=layer, w_layer=layer, which_shift=3, which_scale=4,
                       cols=D_FF, tn=1024, epilogue="relu2")
    half = D_FF // 2
    for k_off in (0, half):
        x = matmul(a, mlp_w2, layer=layer, col_off=0, cols=D_MODEL, tn=512, k_off=k_off, k_len=half,
                   out_dtype=F32, epilogue="resid", x=x, modt=modt, mod_layer=layer, which=5)
    return x


def kernel(x_prompt, x_sample, cache_attn_k, cache_attn_v, state_ssm, c, c_ctx, mod_w, mod_b, norm_mix, norm_mlp, mlp_w1, mlp_w2, na_wqkv, na_wo, na_rpb, ssm_w_in, ssm_conv_w, ssm_conv_b, ssm_dt_bias, ssm_a_log, ssm_d, ssm_norm, ssm_w_out, final_norm):
    x_ctx = x_prompt.reshape(N_CTX, D_MODEL)
    x_lat = x_sample.reshape(N_LAT, D_MODEL)
    cond = jnp.concatenate([c_ctx[None, :], c, jnp.zeros((COND_ROWS - 1 - DEC_BATCH, D_MODEL), F32)], axis=0)
    modt = mod_table(cond, mod_w, mod_b)

    h_ctx = norm_mod(x_ctx, norm_mix, modt, 0, 0, 1)
    h_lat = norm_mod(x_lat, norm_mix, modt, 0, 0, 1, tok_off=N_CTX)
    qkv = functools.partial(matmul, w=na_wqkv, layer=0, cols=D_MODEL, tn=1024)
    q_scale = (NA_HEAD_DIM ** -0.5, D_MODEL // 1024)
    q_ctx = qkv(h_ctx, col_off=0, out_dtype=BF16, scale=q_scale, tm=2048)
    k_ctx = qkv(h_ctx, col_off=D_MODEL, out_dtype=F32)
    v_ctx = qkv(h_ctx, col_off=2 * D_MODEL, out_dtype=F32)
    qkv_lat = matmul(h_lat, na_wqkv, layer=0, col_off=0, cols=3 * D_MODEL, tn=1024, tm=2048, out_dtype=BF16,
                     scale=q_scale)
    o_ctx = ctx_attention(q_ctx, k_ctx, v_ctx)
    n_attn = cache_attn_k.shape[1]
    o_lat = neighborhood_attention(
        qkv_lat,
        cache_attn_k.reshape(DEC_BATCH, n_attn * PAST_LEN, D_MODEL),
        cache_attn_v.reshape(DEC_BATCH, n_attn * PAST_LEN, D_MODEL),
        0, _na_bias_table(na_rpb[0]))
    x = matmul((o_ctx, o_lat), na_wo, layer=0, col_off=0, cols=D_MODEL, tn=1024, tm=512, out_dtype=F32,
               epilogue="resid", x=(x_ctx, x_lat), modt=modt, mod_layer=0, which=2)
    x = _mlp(x, modt, 0, norm_mlp, mlp_w1, mlp_w2)

    zxbc_cols = SSM_D_INNER + SSM_CONV_DIM
    proj, h = norm_matmul(x, norm_mix, modt, ssm_w_in, layer=1, w_layer=0, which_shift=0, which_scale=1,
                          cols=zxbc_cols, tn=1024)
    dt_raw = matmul(h, ssm_w_in, layer=0, col_off=zxbc_cols, cols=2 * SSM_HEADS, tn=128, out_dtype=F32)
    e_n = HEADS_PER_GROUP
    dt_hm = dt_raw.reshape(N_TOK // SSM_CHUNK, SSM_CHUNK, 2, SSM_GROUPS, e_n).transpose(3, 0, 2, 4, 1)
    dt_hm = dt_hm.reshape(SSM_GROUPS, N_TOK // SSM_CHUNK, 2 * e_n, SSM_CHUNK)
    by_group = lambda p: p.reshape(2, SSM_GROUPS, e_n).transpose(1, 0, 2).reshape(SSM_GROUPS, 2 * e_n, 1)
    n_ssm = state_ssm.shape[1]
    ssd = functools.partial(
        ssd_scan, proj, dt_hm, ssm_conv_w, ssm_conv_b.reshape(1, 1, SSM_CONV_DIM),
        by_group(ssm_dt_bias[0]), by_group(ssm_a_log[0]), jnp.repeat(ssm_d[0], SSM_HEAD_DIM, axis=1))
    y_ctx, new_state = ssd(seq_len=SEQ, n_seq=BATCH, row_off=0, seqs_per_step=8)
    y_lat = ssd(seq_len=DEC_SEQ, n_seq=DEC_BATCH, row_off=N_CTX,
                h0=state_ssm.reshape(DEC_BATCH, n_ssm * 2, SSM_D_INNER, SSM_STATE))
    yn = gate_norm(y_ctx, y_lat, proj, ssm_norm)
    x = matmul(yn, ssm_w_out, layer=0, col_off=0, cols=D_MODEL, tn=512, out_dtype=F32,
               epilogue="resid", x=x, modt=modt, mod_layer=1, which=2)
    x = _mlp(x, modt, 1, norm_mlp, mlp_w1, mlp_w2)

    y_prompt = final_rmsnorm(x, final_norm, 0, N_CTX).reshape(BATCH, SEQ, D_MODEL)
    y_sample = final_rmsnorm(x, final_norm, N_CTX, N_LAT).reshape(DEC_BATCH, DEC_SEQ, D_MODEL)
    kv_shape = (BATCH, 1, SEQ, NA_HEADS, NA_HEAD_DIM)
    return (y_prompt, y_sample, k_ctx.reshape(kv_shape), v_ctx.reshape(kv_shape),
            new_state.reshape(BATCH, 1, 2, SSM_HEADS, SSM_HEAD_DIM, SSM_STATE))
```

```python
import functools

import jax
import jax.numpy as jnp
import numpy as np
from jax import lax
from jax.experimental import pallas as pl
from jax.experimental.pallas import tpu as pltpu

F32 = jnp.float32
BF16 = jnp.bfloat16

D_MODEL = 2048
BATCH = 32
SEQ = 256
DEPTH = 2
DEC_BATCH = 4
DEC_SEQ = 1024
PAST_LEN = 256
GRID_W = 64
NA_HEADS = 16
NA_HEAD_DIM = 128
WIN_ROWS = 8
WIN_COLS = 16
SSM_D_INNER = 4096
SSM_HEAD_DIM = 64
SSM_HEADS = 64
SSM_GROUPS = 8
SSM_STATE = 128
SSM_CHUNK = 128
SSM_CONV_DIM = SSM_D_INNER + 2 * SSM_GROUPS * SSM_STATE
D_FF = 4 * D_MODEL
NORM_EPS = 1e-6
NEG_INF = -1e30
LOG2E = 1.4426950408889634

N_CTX = BATCH * SEQ
N_LAT = DEC_BATCH * DEC_SEQ
N_TOK = N_CTX + N_LAT
COND_ROWS = 8
HEADS_PER_GROUP = SSM_HEADS // SSM_GROUPS
GROUP_WIDTH = HEADS_PER_GROUP * SSM_HEAD_DIM

VMEM_LIMIT_BYTES = 56 * 1024 * 1024


def _params(n_axes):
    return pltpu.CompilerParams(dimension_semantics=("arbitrary",) * n_axes,
                                vmem_limit_bytes=VMEM_LIMIT_BYTES)


def _silu(v):
    h = 0.5 * v
    return h + h * jnp.tanh(h)


def _softplus(x):
    return jnp.maximum(x, 0.0) + jnp.log(1.0 + jnp.exp(-jnp.abs(x)))


def _cond_row(row_start):
    return jnp.where(row_start < N_CTX, 0, 1 + (row_start - N_CTX) // DEC_SEQ)


def _mod_kernel(cond_ref, w_ref, b_ref, o_ref):
    s = _silu(cond_ref[...]).astype(BF16)
    part = jnp.dot(s, w_ref[...].astype(BF16), preferred_element_type=F32)

    @pl.when(pl.program_id(1) == 0)
    def _():
        o_ref[...] = part + b_ref[...]

    @pl.when(pl.program_id(1) > 0)
    def _():
        o_ref[...] += part


def mod_table(cond, mod_w, mod_b):
    tk = 256
    n6 = 6 * D_MODEL
    out = pl.pallas_call(
        _mod_kernel,
        grid=(DEPTH, D_MODEL // tk),
        in_specs=[pl.BlockSpec((COND_ROWS, tk), lambda l, k: (0, k)),
                  pl.BlockSpec((None, tk, n6), lambda l, k: (l, k, 0)),
                  pl.BlockSpec((None, 1, n6), lambda l, k: (l, 0, 0))],
        out_specs=pl.BlockSpec((None, COND_ROWS, n6), lambda l, k: (l, 0, 0)),
        out_shape=jax.ShapeDtypeStruct((DEPTH, COND_ROWS, n6), F32),
        compiler_params=_params(2),
        name="mod_table",
    )(cond, mod_w, mod_b.reshape(DEPTH, 1, n6))
    return out.reshape(DEPTH, COND_ROWS, 6, 1, D_MODEL)


def _norm_mod_kernel(x_ref, g_ref, shift_ref, scale_ref, o_ref):
    x = x_ref[...]
    ms = jnp.mean(x * x, axis=-1, keepdims=True)
    y = x * lax.rsqrt(ms + NORM_EPS) * g_ref[...]
    o_ref[...] = (y * (1.0 + scale_ref[...]) + shift_ref[...]).astype(o_ref.dtype)


def norm_mod(x, gain, modt, layer, which_shift, which_scale, tok_off=0, tm=1024):
    n = x.shape[0]
    mod_spec = lambda which: pl.BlockSpec(
        (None, None, None, 1, D_MODEL), lambda m: (layer, _cond_row(tok_off + m * tm), which, 0, 0))
    return pl.pallas_call(
        _norm_mod_kernel,
        grid=(n // tm,),
        in_specs=[pl.BlockSpec((tm, D_MODEL), lambda m: (m, 0)),
                  pl.BlockSpec((None, 1, D_MODEL), lambda m: (layer, 0, 0)),
                  mod_spec(which_shift), mod_spec(which_scale)],
        out_specs=pl.BlockSpec((tm, D_MODEL), lambda m: (m, 0)),
        out_shape=jax.ShapeDtypeStruct((n, D_MODEL), BF16),
        compiler_params=_params(1),
        name="norm_mod",
    )(x, gain.reshape(-1, 1, D_MODEL), modt, modt)


def _final_norm_kernel(x_ref, g_ref, o_ref):
    x = x_ref[...]
    ms = jnp.mean(x * x, axis=-1, keepdims=True)
    o_ref[...] = x * lax.rsqrt(ms + NORM_EPS) * g_ref[...]


def final_rmsnorm(x, gain, row_off, rows, tm=1024):
    off = row_off // tm
    return pl.pallas_call(
        _final_norm_kernel,
        grid=(rows // tm,),
        in_specs=[pl.BlockSpec((tm, D_MODEL), lambda m: (off + m, 0)),
                  pl.BlockSpec((1, D_MODEL), lambda m: (0, 0))],
        out_specs=pl.BlockSpec((tm, D_MODEL), lambda m: (m, 0)),
        out_shape=jax.ShapeDtypeStruct((rows, D_MODEL), F32),
        compiler_params=_params(1),
        name="final_norm",
    )(x, gain.reshape(1, D_MODEL))


def _mm_kernel(*refs, epilogue, scale, n_src, split):
    refs = list(refs)
    a_refs = [refs.pop(0) for _ in range(n_src)]
    w_ref = refs.pop(0)
    resid = epilogue == "resid"
    x_refs = [refs.pop(0) for _ in range(n_src)] if resid else [None] * n_src
    gate_ref = refs.pop(0) if resid else None
    o_ref, wb_ref = refs
    m = pl.program_id(1)

    @pl.when(m == 0)
    def _():
        wb_ref[...] = w_ref[...].astype(BF16)

    def step(a_ref, x_ref):
        acc = jnp.dot(a_ref[...], wb_ref[...], preferred_element_type=F32)
        if epilogue == "cast":
            if scale is not None:
                acc = jnp.where(pl.program_id(0) < scale[1], acc * scale[0], acc)
            o_ref[...] = acc.astype(o_ref.dtype)
        elif epilogue == "relu2":
            r = jnp.maximum(acc, 0.0)
            o_ref[...] = (r * r).astype(o_ref.dtype)
        else:
            o_ref[...] = x_ref[...] + gate_ref[...] * acc

    if n_src == 1:
        step(a_refs[0], x_refs[0])
    else:
        pl.when(m < split)(lambda: step(a_refs[0], x_refs[0]))
        pl.when(m >= split)(lambda: step(a_refs[1], x_refs[1]))


def matmul(a, w, *, layer, col_off, cols, tn, out_dtype, tm=1024, k_off=0, k_len=None, epilogue="cast",
           scale=None, x=None, modt=None, mod_layer=None, which=None):
    a_parts = a if isinstance(a, tuple) else (a,)
    x_parts = () if x is None else (x if isinstance(x, tuple) else (x,))
    tk = a_parts[0].shape[1] if k_len is None else k_len
    rows = sum(p.shape[0] for p in a_parts)
    n_off, kb_off = col_off // tn, k_off // tk
    split = N_CTX // tm

    def row_specs(parts, width, col_block):
        if len(parts) == 1:
            return [pl.BlockSpec((tm, width), lambda n, m: (m, col_block(n)))]
        return [pl.BlockSpec((tm, width), lambda n, m: (jnp.minimum(m, split - 1), col_block(n))),
                pl.BlockSpec((tm, width), lambda n, m: (jnp.maximum(m - split, 0), col_block(n)))]

    in_specs = row_specs(a_parts, tk, lambda n: kb_off)
    in_specs.append(pl.BlockSpec((None, tk, tn), lambda n, m: (layer, kb_off, n_off + n)))
    args = list(a_parts) + [w]
    if epilogue == "resid":
        assert len(x_parts) == len(a_parts)
        in_specs += row_specs(x_parts, tn, lambda n: n)
        in_specs.append(pl.BlockSpec((None, None, None, 1, tn),
                                     lambda n, m: (mod_layer, _cond_row(m * tm), which, 0, n)))
        args += list(x_parts) + [modt]
    return pl.pallas_call(
        functools.partial(_mm_kernel, epilogue=epilogue, scale=scale, n_src=len(a_parts), split=split),
        grid=(cols // tn, rows // tm),
        in_specs=in_specs,
        out_specs=pl.BlockSpec((tm, tn), lambda n, m: (m, n)),
        out_shape=jax.ShapeDtypeStruct((rows, cols), out_dtype),
        scratch_shapes=[pltpu.VMEM((tk, tn), BF16)],
        compiler_params=_params(2),
        name="matmul_" + epilogue,
    )(*args)


CTX_REQS_PER_STEP = 2


def _ctx_attn_kernel(q_ref, k_ref, v_ref, o_ref):
    for r in range(CTX_REQS_PER_STEP):
        rows = slice(r * SEQ, (r + 1) * SEQ)
        for h in range(NA_HEADS):
            sl = slice(h * NA_HEAD_DIM, (h + 1) * NA_HEAD_DIM)
            q = q_ref[rows, sl]
            k = k_ref[rows, sl].astype(BF16)
            v = v_ref[rows, sl].astype(BF16)
            s = lax.dot_general(q, k, (((1,), (1,)), ((), ())), preferred_element_type=F32)
            p = jnp.exp(s - jnp.max(s, axis=-1, keepdims=True))
            l = jnp.sum(p, axis=-1, keepdims=True)
            o = jnp.dot(p.astype(BF16), v, preferred_element_type=F32) / l
            o_ref[rows, sl] = o.astype(o_ref.dtype)


def ctx_attention(q, k, v):
    spec = pl.BlockSpec((CTX_REQS_PER_STEP * SEQ, D_MODEL), lambda b: (b, 0))
    return pl.pallas_call(
        _ctx_attn_kernel,
        grid=(BATCH // CTX_REQS_PER_STEP,),
        in_specs=[spec, spec, spec],
        out_specs=spec,
        out_shape=jax.ShapeDtypeStruct((N_CTX, D_MODEL), BF16),
        compiler_params=_params(1),
        name="ctx_attention",
    )(q, k, v)


GRID_ROWS = DEC_SEQ // GRID_W
Q_BLOCK_ROWS = 4
KEY_WIN_ROWS = 12
N_ROW_OFFS = 2 * WIN_ROWS - 1
TILE_BOTH, TILE_SECOND, TILE_FIRST = 0, N_ROW_OFFS - 1, 2 * N_ROW_OFFS - 1
TILE_NONE = 3 * N_ROW_OFFS - 1
N_BIAS_TILES = TILE_NONE + 1


def _row_start(r):
    return min(max(r - WIN_ROWS // 2, 0), GRID_ROWS - WIN_ROWS)


def _key_win_start(rb):
    return min(max(rb * Q_BLOCK_ROWS - WIN_ROWS // 2, 0), GRID_ROWS - KEY_WIN_ROWS)


def _na_bias_table(rpb):
    reach = WIN_COLS - 1
    qc = np.arange(GRID_W)[:, None]
    kc = np.arange(GRID_W)[None, :]
    ws = np.clip(qc - WIN_COLS // 2, 0, GRID_W - WIN_COLS)
    in_window = (kc >= ws) & (kc < ws + WIN_COLS)
    col_off = np.clip(kc - qc, -reach, reach) + reach
    pick = ((np.arange(2 * reach + 1)[:, None, None] == col_off[None]) & in_window[None]).astype(np.float32)
    t = jnp.einsum("hdk,kqc->hdqc", rpb, pick, precision=lax.Precision.HIGHEST)
    t = jnp.where(in_window, t, NEG_INF)
    masked = jnp.full_like(t, NEG_INF)
    return jnp.concatenate([
        jnp.concatenate([t[:, :-1], t[:, 1:]], axis=-1),
        jnp.concatenate([masked, t], axis=-1),
        jnp.concatenate([t, masked], axis=-1),
        jnp.concatenate([masked[:, :1], masked[:, :1]], axis=-1)], axis=1)


def _bias_tile_index(r, key_row):
    rs = _row_start(r)
    first_ok = rs <= key_row < rs + WIN_ROWS
    second_ok = rs <= key_row + 1 < rs + WIN_ROWS
    off = key_row - r + WIN_ROWS - 1
    if first_ok and second_ok:
        return TILE_BOTH + off
    if second_ok:
        return TILE_SECOND + off + 1
    if first_ok:
        return TILE_FIRST + off
    return TILE_NONE


def _na_kernel(q_ref, k_ref, v_ref, ck_ref, cv_ref, t_ref, o_ref):
    ck = ck_ref[...].astype(BF16)
    cv = cv_ref[...].astype(BF16)
    nt = (((1,), (1,)), ((), ()))
    blk = Q_BLOCK_ROWS * GRID_W
    for rb in range(GRID_ROWS // Q_BLOCK_ROWS):
        kw0 = _key_win_start(rb)
        keys = slice(kw0 * GRID_W, (kw0 + KEY_WIN_ROWS) * GRID_W)
        q = q_ref[rb * blk:(rb + 1) * blk, :]
        bias = jnp.concatenate([
            jnp.concatenate([t_ref[_bias_tile_index(rb * Q_BLOCK_ROWS + i, kw0 + 2 * p)]
                             for p in range(KEY_WIN_ROWS // 2)], axis=1)
            for i in range(Q_BLOCK_ROWS)], axis=0)
        s_loc = lax.dot_general(q, k_ref[keys, :], nt, preferred_element_type=F32) + bias
        s_ctx = lax.dot_general(q, ck, nt, preferred_element_type=F32)
        mx = jnp.maximum(jnp.max(s_loc, axis=-1, keepdims=True), jnp.max(s_ctx, axis=-1, keepdims=True))
        p_loc = jnp.exp(s_loc - mx)
        p_ctx = jnp.exp(s_ctx - mx)
        l = jnp.sum(p_loc, axis=-1, keepdims=True) + jnp.sum(p_ctx, axis=-1, keepdims=True)
        o = (jnp.dot(p_loc.astype(BF16), v_ref[keys, :], preferred_element_type=F32)
             + jnp.dot(p_ctx.astype(BF16), cv, preferred_element_type=F32)) / l
        o_ref[rb * blk:(rb + 1) * blk, :] = o.astype(o_ref.dtype)


def neighborhood_attention(qkv, cache_k, cache_v, cache_layer, bias_table):
    tok = lambda part: pl.BlockSpec((DEC_SEQ, NA_HEAD_DIM), lambda h, b: (b, part * NA_HEADS + h))
    cache = pl.BlockSpec((None, PAST_LEN, NA_HEAD_DIM), lambda h, b: (b, cache_layer, h))
    return pl.pallas_call(
        _na_kernel,
        grid=(NA_HEADS, DEC_BATCH),
        in_specs=[tok(0), tok(1), tok(2), cache, cache,
                  pl.BlockSpec((None, N_BIAS_TILES, GRID_W, 2 * GRID_W), lambda h, b: (h, 0, 0, 0))],
        out_specs=tok(0),
        out_shape=jax.ShapeDtypeStruct((N_LAT, D_MODEL), BF16),
        compiler_params=_params(2),
        name="neighborhood_attention",
    )(qkv, qkv, qkv, cache_k, cache_v, bias_table)


N_SSD_INPUTS = 13


def _ssd_kernel(*refs, seq_len, has_h0, emit_state, seqs_per_step):
    n_in = N_SSD_INPUTS + (1 if has_h0 else 0)
    n_out = 1 + (1 if emit_state else 0)
    ins, outs, scratch = refs[:n_in], refs[n_in:n_in + n_out], refs[n_in + n_out:]
    nc = seq_len // SSM_CHUNK
    for s in range(seqs_per_step):
        tok = pl.ds(s * seq_len, seq_len)
        views = [ins[0].at[tok], ins[1].at[tok], ins[2].at[tok]] + list(ins[3:9])
        views += [ins[9].at[pl.ds(s * nc, nc)]] + list(ins[10:N_SSD_INPUTS])
        if has_h0:
            views.append(ins[N_SSD_INPUTS].at[s])
        views.append(outs[0].at[tok])
        if emit_state:
            views.append(outs[1].at[s])
        views += [r.at[s] for r in scratch]
        _ssd_sequence(*views, seq_len=seq_len, has_h0=has_h0, emit_state=emit_state)


def _ssd_sequence(*refs, seq_len, has_h0, emit_state):
    (x_ref, b_ref, c_ref, cwx_ref, cwb_ref, cwc_ref, cbx_ref, cbb_ref, cbc_ref,
     dt_ref, bias_ref, alog_ref, dsk_ref) = refs[:N_SSD_INPUTS]
    rest = list(refs[N_SSD_INPUTS:])
    h0_ref = rest.pop(0) if has_h0 else None
    y_ref = rest.pop(0)
    st_ref = rest.pop(0) if emit_state else None
    (xs_ref, bt_ref, cs_ref, act3_ref, rowp_ref, rowd_ref, cdb_ref, wr_ref, sloc_ref, hprev_ref, hcur_ref) = rest

    q = SSM_CHUNK
    nc = seq_len // q
    hp = lax.Precision.HIGHEST
    e_n = HEADS_PER_GROUP
    half = SSM_HEAD_DIM
    n_pairs = e_n // 2

    li = lax.broadcasted_iota(jnp.int32, (q, q), 0)
    si = lax.broadcasted_iota(jnp.int32, (q, q), 1)
    tril = (si <= li).astype(F32)
    triu = (si >= li).astype(F32)
    lower = si <= li
    diag = si == li
    first_half = si < half
    fwd_row = lax.broadcasted_iota(jnp.int32, (2 * e_n, q), 0) < e_n
    a_hm = -jnp.exp(alog_ref[...]) * LOG2E

    def chunk_rows(c):
        return pl.ds(c * q, q) if isinstance(c, int) else pl.ds(pl.multiple_of(c * q, q), q)

    def block_diag(pair):
        zero = jnp.zeros_like(pair)
        return jnp.concatenate([jnp.where(first_half, pair, zero), jnp.where(first_half, zero, pair)], axis=0)

    dt_all = _softplus(dt_ref[...] + bias_ref[...])
    da_all = (dt_all * a_hm).reshape(nc * 2 * e_n, q)
    fwd_rows = (lax.broadcasted_iota(jnp.int32, (nc * 2 * e_n, q), 0) % (2 * e_n)) < e_n
    ac_all = jnp.where(fwd_rows,
                       jnp.dot(da_all, triu, precision=hp, preferred_element_type=F32),
                       jnp.dot(da_all, tril, precision=hp, preferred_element_type=F32)).reshape(nc, 2 * e_n, q)
    rowp_ref[...] = ac_all - jnp.log2(dt_all)
    rowd_ref[...] = jnp.log2(dt_all[:, :e_n, :] + dt_all[:, e_n:, :])
    ends = jnp.broadcast_to(jnp.where(fwd_row[None, :, 0:1], ac_all[:, :, q - 1:q], ac_all[:, :, 0:1]), ac_all.shape)
    wr_ref[...] = jnp.exp2(ends - ac_all) * dt_all
    cdb_ref[...] = jnp.exp2(ends)
    hi = ac_all.astype(BF16).astype(F32)
    mid = (ac_all - hi).astype(BF16).astype(F32)
    lo = (ac_all - hi - mid).astype(BF16).astype(F32)
    for c in range(nc):
        act3_ref[c * q:(c + 1) * q, :] = jnp.concatenate([hi[c], mid[c], lo[c]], axis=0).T.astype(BF16)

    def prep(c, carry):
        rows = chunk_rows(c)
        r0 = c * q

        def conv_silu(u_ref, w_ref, bias_r):
            u = u_ref[rows, :].astype(F32)
            width = u.shape[1]
            before = u_ref[pl.ds(pl.multiple_of(jnp.maximum(r0 - 16, 0), 16), 16), :][15:16, :].astype(F32)
            after = u_ref[pl.ds(pl.multiple_of(jnp.minimum(r0 + q, seq_len - 16), 16), 16), :][0:1, :].astype(F32)
            before = jnp.where(c > 0, before, 0.0)
            after = jnp.where(c < nc - 1, after, 0.0)
            rid = lax.broadcasted_iota(jnp.int32, (q, width), 0)
            prev = jnp.where(rid == 0, before, pltpu.roll(u, 1, axis=0))
            nxt = jnp.where(rid == q - 1, after, pltpu.roll(u, q - 1, axis=0))
            w = w_ref[...] * 0.5
            h = prev * w[0:1] + u * w[1:2] + nxt * w[2:3] + bias_r[...] * 0.5
            return h + h * jnp.tanh(h)

        xc = conv_silu(x_ref, cwx_ref, cbx_ref).astype(BF16)
        xs_ref[rows, :] = xc
        cs_ref[rows, :] = conv_silu(c_ref, cwc_ref, cbc_ref).astype(BF16)
        bt = conv_silu(b_ref, cwb_ref, cbb_ref).T
        bt_ref[c] = bt
        wr = wr_ref[c]

        for j in range(n_pairs):
            lanes = slice(j * 2 * half, (j + 1) * 2 * half)
            xblk = block_diag(xc[:, lanes])
            e1, e2 = 2 * j, 2 * j + 1
            scaled = lambda r: (bt * wr[r:r + 1, :]).astype(BF16)
            lhs = jnp.concatenate([
                jnp.concatenate([scaled(e1), scaled(e2)], axis=1),
                jnp.concatenate([scaled(e_n + e1), scaled(e_n + e2)], axis=1)], axis=0)
            out = jnp.dot(lhs, xblk, preferred_element_type=F32)
            sloc_ref[c, 0, :, lanes] = out[:q]
            sloc_ref[c, 1, :, lanes] = out[q:]
        return carry

    unrolled = nc <= 2
    if unrolled:
        for c in range(nc):
            prep(c, 0)
    else:
        lax.fori_loop(0, nc, prep, 0, unroll=2)

    if has_h0:
        hcur_ref[0] = h0_ref[0].T
        hcur_ref[1] = h0_ref[1].T
    else:
        hcur_ref[...] = jnp.zeros(hcur_ref.shape, F32)

    def carry_states(direction):
        def body(i, carry):
            c = i if direction == 0 else nc - 1 - i
            h = hcur_ref[direction]
            hprev_ref[c, direction] = h.astype(BF16)
            cd = cdb_ref[c]
            r0 = direction * e_n
            decay = jnp.concatenate(
                [jnp.where(first_half[0:1, :], cd[r0 + 2 * j:r0 + 2 * j + 1, :], cd[r0 + 2 * j + 1:r0 + 2 * j + 2, :])
                 for j in range(n_pairs)], axis=1)
            hcur_ref[direction] = h * decay + sloc_ref[c, direction]
            return carry
        if unrolled:
            for i in range(nc):
                body(i, 0)
        else:
            lax.fori_loop(0, nc, body, 0)

    carry_states(0)
    carry_states(1)
    if emit_state:
        st_ref[0] = hcur_ref[0].T
        st_ref[1] = hcur_ref[1].T

    dsum = dsk_ref[0:1, :] + dsk_ref[1:2, :]
    sel_k = lax.broadcasted_iota(jnp.int32, (3 * 2 * e_n, 4 * q), 0) % (2 * e_n)
    sel_n = lax.broadcasted_iota(jnp.int32, (3 * 2 * e_n, 4 * q), 1) // q
    spread_sel = [(sel_k == 2 * j + (sel_n % 2) + e_n * (sel_n // 2)).astype(BF16) for j in range(n_pairs)]

    def emit(c, zero_in_f=False, zero_in_b=False):
        rows = chunk_rows(c)
        cc = cs_ref[rows, :]
        cb = jnp.dot(cc, bt_ref[c].astype(BF16), preferred_element_type=F32)
        off_f = None if zero_in_f else jnp.dot(cc, hprev_ref[c, 0], preferred_element_type=F32)
        off_b = None if zero_in_b else jnp.dot(cc, hprev_ref[c, 1], preferred_element_type=F32)
        act3 = act3_ref[rows, :]
        rp = rowp_ref[c]
        rd = rowd_ref[c]
        spreads = [jnp.dot(act3, spread_sel[j], preferred_element_type=F32) for j in range(n_pairs)]
        for j in range(n_pairs):
            lanes = slice(j * 2 * half, (j + 1) * 2 * half)
            spread = spreads[j]
            acf = [spread[:, 0:q], spread[:, q:2 * q]]
            acb = [spread[:, 2 * q:3 * q], spread[:, 3 * q:4 * q]]
            mix = []
            for i, e in enumerate((2 * j, 2 * j + 1)):
                arg = jnp.where(diag, rd[e:e + 1, :],
                                jnp.where(lower, acf[i] - rp[e:e + 1, :], acb[i] - rp[e_n + e:e_n + e + 1, :]))
                mix.append((cb * jnp.exp2(arg)).astype(BF16))
            xp = xs_ref[rows, lanes]
            y = jnp.dot(jnp.concatenate(mix, axis=1), block_diag(xp), preferred_element_type=F32)
            if off_f is not None:
                y = y + off_f[:, lanes] * jnp.exp2(jnp.where(first_half, acf[0], acf[1]))
            if off_b is not None:
                y = y + off_b[:, lanes] * jnp.exp2(jnp.where(first_half, acb[0], acb[1]))
            y_ref[rows, lanes] = (y + xp.astype(F32) * dsum[:, lanes]).astype(y_ref.dtype)

    if has_h0:
        def emit_body(c, carry):
            emit(c)
            return carry
        lax.fori_loop(0, nc, emit_body, 0, unroll=2)
    else:
        for c in range(nc):
            emit(c, zero_in_f=(c == 0), zero_in_b=(c == nc - 1))


def ssd_scan(proj, dt_hm, conv_w, conv_b, bias_hm, alog_hm, d_skip_wide, *, seq_len, n_seq, row_off,
             seqs_per_step=1, h0=None):
    has_h0 = h0 is not None
    emit_state = not has_h0
    nc = seq_len // SSM_CHUNK
    sps = seqs_per_step
    s_off = row_off // (sps * seq_len)
    gw = GROUP_WIDTH
    ns = SSM_STATE
    e2 = 2 * HEADS_PER_GROUP
    x_blk0 = SSM_D_INNER // gw
    b_blk0 = 2 * SSM_D_INNER // ns
    c_blk0 = b_blk0 + SSM_GROUPS
    cw_b0 = SSM_D_INNER // ns
    in_specs = [
        pl.BlockSpec((sps * seq_len, gw), lambda s, g: (s_off + s, x_blk0 + g)),
        pl.BlockSpec((sps * seq_len, ns), lambda s, g: (s_off + s, b_blk0 + g)),
        pl.BlockSpec((sps * seq_len, ns), lambda s, g: (s_off + s, c_blk0 + g)),
        pl.BlockSpec((None, 3, gw), lambda s, g: (0, 0, g)),
        pl.BlockSpec((None, 3, ns), lambda s, g: (0, 0, cw_b0 + g)),
        pl.BlockSpec((None, 3, ns), lambda s, g: (0, 0, cw_b0 + SSM_GROUPS + g)),
        pl.BlockSpec((None, 1, gw), lambda s, g: (0, 0, g)),
        pl.BlockSpec((None, 1, ns), lambda s, g: (0, 0, cw_b0 + g)),
        pl.BlockSpec((None, 1, ns), lambda s, g: (0, 0, cw_b0 + SSM_GROUPS + g)),
        pl.BlockSpec((None, sps * nc, e2, SSM_CHUNK), lambda s, g: (g, s_off + s, 0, 0)),
        pl.BlockSpec((None, e2, 1), lambda s, g: (g, 0, 0)),
        pl.BlockSpec((None, e2, 1), lambda s, g: (g, 0, 0)),
        pl.BlockSpec((2, gw), lambda s, g: (0, g)),
    ]
    args = [proj, proj, proj, conv_w, conv_w, conv_w, conv_b, conv_b, conv_b,
            dt_hm, bias_hm, alog_hm, d_skip_wide]
    assert len(args) == N_SSD_INPUTS
    state_spec = pl.BlockSpec((sps, 2, gw, ns), lambda s, g: (s, 0, g, 0))
    if has_h0:
        in_specs.append(state_spec)
        args.append(h0)
    y_spec = pl.BlockSpec((sps * seq_len, gw), lambda s, g: (s, g))
    y_shape = jax.ShapeDtypeStruct((n_seq * seq_len, SSM_D_INNER), BF16)
    if emit_state:
        out_specs = [y_spec, state_spec]
        out_shape = [y_shape, jax.ShapeDtypeStruct((n_seq, 2, SSM_D_INNER, ns), F32)]
    else:
        out_specs = y_spec
        out_shape = y_shape
    return pl.pallas_call(
        functools.partial(_ssd_kernel, seq_len=seq_len, has_h0=has_h0, emit_state=emit_state,
                          seqs_per_step=sps),
        grid=(n_seq // sps, SSM_GROUPS),
        in_specs=in_specs,
        out_specs=out_specs,
        out_shape=out_shape,
        scratch_shapes=[
            pltpu.VMEM((sps, seq_len, gw), BF16),
            pltpu.VMEM((sps, nc, ns, SSM_CHUNK), F32),
            pltpu.VMEM((sps, seq_len, ns), BF16),
            pltpu.VMEM((sps, seq_len, 3 * e2), BF16),
            pltpu.VMEM((sps, nc, e2, SSM_CHUNK), F32),
            pltpu.VMEM((sps, nc, HEADS_PER_GROUP, SSM_CHUNK), F32),
            pltpu.VMEM((sps, nc, e2, SSM_CHUNK), F32),
            pltpu.VMEM((sps, nc, e2, SSM_CHUNK), F32),
            pltpu.VMEM((sps, nc, 2, ns, gw), F32),
            pltpu.VMEM((sps, nc, 2, ns, gw), BF16),
            pltpu.VMEM((sps, 2, ns, gw), F32)],
        compiler_params=_params(2),
        name="ssd_scan",
    )(*args)


def _gate_norm_kernel(y_ctx_ref, y_lat_ref, z_ref, g_ref, o_ref, *, split):
    y = jnp.where(pl.program_id(0) < split, y_ctx_ref[...], y_lat_ref[...])
    v = y.astype(F32) * _silu(z_ref[...].astype(F32))
    ms = jnp.mean(v * v, axis=-1, keepdims=True)
    o_ref[...] = (v * lax.rsqrt(ms + NORM_EPS) * g_ref[...]).astype(o_ref.dtype)


def gate_norm(y_ctx, y_lat, proj, gain, tm=512):
    split = N_CTX // tm
    wide = lambda index: pl.BlockSpec((tm, SSM_D_INNER), index)
    return pl.pallas_call(
        functools.partial(_gate_norm_kernel, split=split),
        grid=(N_TOK // tm,),
        in_specs=[wide(lambda m: (jnp.minimum(m, split - 1), 0)),
                  wide(lambda m: (jnp.maximum(m - split, 0), 0)),
                  wide(lambda m: (m, 0)),
                  pl.BlockSpec((1, SSM_D_INNER), lambda m: (0, 0))],
        out_specs=wide(lambda m: (m, 0)),
        out_shape=jax.ShapeDtypeStruct((N_TOK, SSM_D_INNER), BF16),
        compiler_params=_params(1),
        name="gate_norm",
    )(y_ctx, y_lat, proj, gain.reshape(1, SSM_D_INNER))


def _mlp(x, modt, layer, norm_mlp, mlp_w1, mlp_w2):
    h = norm_mod(x, norm_mlp, modt, layer, 3, 4)
    a = matmul(h, mlp_w1, layer=layer, col_off=0, cols=D_FF, tn=1024, tm=2048, out_dtype=BF16, epilogue="relu2")
    half = D_FF // 2
    for k_off in (0, half):
        x = matmul(a, mlp_w2, layer=layer, col_off=0, cols=D_MODEL, tn=512, k_off=k_off, k_len=half,
                   out_dtype=F32, epilogue="resid", x=x, modt=modt, mod_layer=layer, which=5)
    return x


def kernel(x_prompt, x_sample, cache_attn_k, cache_attn_v, state_ssm, c, c_ctx, mod_w, mod_b, norm_mix, norm_mlp, mlp_w1, mlp_w2, na_wqkv, na_wo, na_rpb, ssm_w_in, ssm_conv_w, ssm_conv_b, ssm_dt_bias, ssm_a_log, ssm_d, ssm_norm, ssm_w_out, final_norm):
    x_ctx = x_prompt.reshape(N_CTX, D_MODEL)
    x_lat = x_sample.reshape(N_LAT, D_MODEL)
    cond = jnp.concatenate([c_ctx[None, :], c, jnp.zeros((COND_ROWS - 1 - DEC_BATCH, D_MODEL), F32)], axis=0)
    modt = mod_table(cond, mod_w, mod_b)

    h_ctx = norm_mod(x_ctx, norm_mix, modt, 0, 0, 1)
    h_lat = norm_mod(x_lat, norm_mix, modt, 0, 0, 1, tok_off=N_CTX)
    qkv = functools.partial(matmul, w=na_wqkv, layer=0, cols=D_MODEL, tn=1024)
    q_scale = (NA_HEAD_DIM ** -0.5, D_MODEL // 1024)
    q_ctx = qkv(h_ctx, col_off=0, out_dtype=BF16, scale=q_scale, tm=2048)
    k_ctx = qkv(h_ctx, col_off=D_MODEL, out_dtype=F32)
    v_ctx = qkv(h_ctx, col_off=2 * D_MODEL, out_dtype=F32)
    qkv_lat = matmul(h_lat, na_wqkv, layer=0, col_off=0, cols=3 * D_MODEL, tn=1024, tm=2048, out_dtype=BF16,
                     scale=q_scale)
    o_ctx = ctx_attention(q_ctx, k_ctx, v_ctx)
    n_attn = cache_attn_k.shape[1]
    o_lat = neighborhood_attention(
        qkv_lat,
        cache_attn_k.reshape(DEC_BATCH, n_attn * PAST_LEN, D_MODEL),
        cache_attn_v.reshape(DEC_BATCH, n_attn * PAST_LEN, D_MODEL),
        0, _na_bias_table(na_rpb[0]))
    x = matmul((o_ctx, o_lat), na_wo, layer=0, col_off=0, cols=D_MODEL, tn=1024, tm=512, out_dtype=F32,
               epilogue="resid", x=(x_ctx, x_lat), modt=modt, mod_layer=0, which=2)
    x = _mlp(x, modt, 0, norm_mlp, mlp_w1, mlp_w2)

    h = norm_mod(x, norm_mix, modt, 1, 0, 1)
    zxbc_cols = SSM_D_INNER + SSM_CONV_DIM
    proj = matmul(h, ssm_w_in, layer=0, col_off=0, cols=zxbc_cols, tn=1024, tm=2048, out_dtype=BF16)
    dt_raw = matmul(h, ssm_w_in, layer=0, col_off=zxbc_cols, cols=2 * SSM_HEADS, tn=128, out_dtype=F32)
    e_n = HEADS_PER_GROUP
    dt_hm = dt_raw.reshape(N_TOK // SSM_CHUNK, SSM_CHUNK, 2, SSM_GROUPS, e_n).transpose(3, 0, 2, 4, 1)
    dt_hm = dt_hm.reshape(SSM_GROUPS, N_TOK // SSM_CHUNK, 2 * e_n, SSM_CHUNK)
    by_group = lambda p: p.reshape(2, SSM_GROUPS, e_n).transpose(1, 0, 2).reshape(SSM_GROUPS, 2 * e_n, 1)
    n_ssm = state_ssm.shape[1]
    ssd = functools.partial(
        ssd_scan, proj, dt_hm, ssm_conv_w, ssm_conv_b.reshape(1, 1, SSM_CONV_DIM),
        by_group(ssm_dt_bias[0]), by_group(ssm_a_log[0]), jnp.repeat(ssm_d[0], SSM_HEAD_DIM, axis=1))
    y_ctx, new_state = ssd(seq_len=SEQ, n_seq=BATCH, row_off=0, seqs_per_step=8)
    y_lat = ssd(seq_len=DEC_SEQ, n_seq=DEC_BATCH, row_off=N_CTX, seqs_per_step=2,
                h0=state_ssm.reshape(DEC_BATCH, n_ssm * 2, SSM_D_INNER, SSM_STATE))
    yn = gate_norm(y_ctx, y_lat, proj, ssm_norm)
    x = matmul(yn, ssm_w_out, layer=0, col_off=0, cols=D_MODEL, tn=512, out_dtype=F32,
               epilogue="resid", x=x, modt=modt, mod_layer=1, which=2)
    x = _mlp(x, modt, 1, norm_mlp, mlp_w1, mlp_w2)

    y_prompt = final_rmsnorm(x, final_norm, 0, N_CTX).reshape(BATCH, SEQ, D_MODEL)
    y_sample = final_rmsnorm(x, final_norm, N_CTX, N_LAT).reshape(DEC_BATCH, DEC_SEQ, D_MODEL)
    kv_shape = (BATCH, 1, SEQ, NA_HEADS, NA_HEAD_DIM)
    return (y_prompt, y_sample, k_ctx.reshape(kv_shape), v_ctx.reshape(kv_shape),
            new_state.reshape(BATCH, 1, 2, SSM_HEADS, SSM_HEAD_DIM, SSM_STATE))
```
